```python
import jax, jax.numpy as jnp
from jax import lax
import numpy as np

D_MODEL = 4096
BATCH = 1
SEQ = 8192
DEPTH = 2
DEC_BATCH = 16
DEC_SEQ = 64
PAST_LEN = 4096

CHUNK = 64
CONV_DIM = 1024
CONV_WIDTH = 3
ATT_HEADS = 8
ATT_HEAD_DIM = 128
ATT_DIM = ATT_HEADS * ATT_HEAD_DIM
BAND_CHUNKS = 8
WINDOW = BAND_CHUNKS * CHUNK
REL_CLIP = 128
GLA_HEADS = 4
GLA_DK = 256
GLA_DV = 512
GLA_KDIM = GLA_HEADS * GLA_DK
GLA_VDIM = GLA_HEADS * GLA_DV
GLA_RANK = 16
GLA_TAU = 16.0
GLA_BLOCK = 16
MIX_DIM = CONV_DIM + ATT_DIM + GLA_VDIM
D_FF = ((8 * D_MODEL // 3 + 255) // 256) * 256
NEG_INF = -1e30
IN_SPLITS = (CONV_DIM, CONV_DIM, CONV_DIM,
             ATT_DIM, ATT_DIM, ATT_DIM,
             GLA_KDIM, GLA_KDIM, GLA_VDIM, GLA_VDIM, GLA_RANK,
             D_MODEL, D_MODEL, D_MODEL)
IN_DIM = sum(IN_SPLITS)

kernel_name = 'hybrid_stream_conv_bandattn_gla_step'


def _split_points():
    return [int(v) for v in np.cumsum(IN_SPLITS)[:-1]]


def rmsnorm(x, g, eps=1e-6):
    xf = x.astype(jnp.float32)
    y = xf * lax.rsqrt(jnp.mean(xf * xf, axis=-1, keepdims=True) + eps)
    return (y * g.astype(jnp.float32)).astype(x.dtype)


def short_conv(b_gate, c_gate, u_in, buf, w):
    u = c_gate * u_in
    t = u.shape[1]
    full = jnp.concatenate([buf.astype(u.dtype), u], axis=1)
    y = full[:, 0:t] * w[0]
    for j in range(1, CONV_WIDTH):
        y = y + full[:, j:j + t] * w[j]
    return b_gate * y, full[:, t:]


def band_attention_prompt(q, k, v, rel_bias):
    n, t, h, dh = q.shape
    nc = t // CHUNK
    span = (BAND_CHUNKS + 1) * CHUNK
    qc = q.reshape(n, nc, CHUNK, h, dh)
    pad = ((0, 0), (WINDOW, 0), (0, 0), (0, 0))
    band = jnp.arange(nc)[:, None] + jnp.arange(BAND_CHUNKS + 1)[None, :]
    kb = jnp.pad(k, pad).reshape(n, nc + BAND_CHUNKS, CHUNK, h, dh)[:, band].reshape(n, nc, span, h, dh)
    vb = jnp.pad(v, pad).reshape(n, nc + BAND_CHUNKS, CHUNK, h, dh)[:, band].reshape(n, nc, span, h, dh)
    s = jnp.einsum('ncqhd,nckhd->nchqk', qc, kb, preferred_element_type=jnp.float32) * (dh ** -0.5)
    j = jnp.arange(span)
    rel = jnp.clip(jnp.arange(CHUNK)[:, None] - j[None, :] + WINDOW, -REL_CLIP, REL_CLIP) + REL_CLIP
    s = s + rel_bias.astype(jnp.float32)[:, rel][None, None]
    kpos = jnp.arange(nc)[:, None] * CHUNK - WINDOW + j[None, :]
    s = jnp.where((kpos >= 0)[None, :, None, None, :], s, NEG_INF)
    p = jax.nn.softmax(s, axis=-1).astype(v.dtype)
    o = jnp.einsum('nchqk,nckhd->ncqhd', p, vb)
    return o.reshape(n, t, h * dh)


def band_attention_sample(q, k, v, k_cache, v_cache, rel_bias):
    n, s_len, h, dh = q.shape
    w = k_cache.shape[1]
    kk = jnp.concatenate([k_cache.astype(k.dtype), k], axis=1)
    vv = jnp.concatenate([v_cache.astype(v.dtype), v], axis=1)
    sc = jnp.einsum('nqhd,nkhd->nhqk', q, kk, preferred_element_type=jnp.float32) * (dh ** -0.5)
    rel = jnp.clip((w + jnp.arange(s_len))[:, None] - jnp.arange(w + s_len)[None, :],
                   -REL_CLIP, REL_CLIP) + REL_CLIP
    sc = sc + rel_bias.astype(jnp.float32)[:, rel][None]
    p = jax.nn.softmax(sc, axis=-1).astype(v.dtype)
    o = jnp.einsum('nhqk,nkhd->nqhd', p, vv)
    return o.reshape(n, s_len, h * dh)


def gla_recurrence(q, k, v, log_a, s0):
    n, t, h, dk = q.shape
    dv = v.shape[-1]
    out_dtype = v.dtype
    tp = -(-t // GLA_BLOCK) * GLA_BLOCK
    nb = tp // GLA_BLOCK

    def blocks(a):
        a = jnp.pad(a.astype(jnp.float32), ((0, 0), (0, tp - t), (0, 0), (0, 0)))
        return a.reshape(n, nb, GLA_BLOCK, h, a.shape[-1]).swapaxes(0, 1)

    causal = jnp.tril(jnp.ones((GLA_BLOCK, GLA_BLOCK), dtype=bool))

    def step(S, blk):
        qb, kb, vb, ab = blk
        b = jnp.cumsum(ab, axis=1)
        b_last = b[:, -1]
        q_dec = qb * jnp.exp(b)
        k_inv = kb * jnp.exp(-b)
        k_rem = kb * jnp.exp(b_last[:, None] - b)
        o_inter = jnp.einsum('nlhk,nhkv->nlhv', q_dec, S)
        att = jnp.where(causal, jnp.einsum('nlhk,nmhk->nhlm', q_dec, k_inv), 0.0)
        o_intra = jnp.einsum('nhlm,nmhv->nlhv', att, vb)
        S = jnp.exp(b_last)[..., None] * S + jnp.einsum('nlhk,nlhv->nhkv', k_rem, vb)
        return S, o_inter + o_intra

    S, o = lax.scan(step, s0.astype(jnp.float32), (blocks(q), blocks(k), blocks(v), blocks(log_a)))
    o = o.swapaxes(0, 1).reshape(n, tp, h, dv)[:, :t]
    return o.astype(out_dtype), S.astype(s0.dtype)


def trunk_layer(x, conv_buf, k_cache, v_cache, gla_s0, g_mix, w_in, conv_w, g_q, g_k, rel_bias,
                w_a2, b_a, g_gla, w_branch, w_out, g_ffn, w_gu, w_down, is_prompt):
    n, t, _ = x.shape
    h = rmsnorm(x, g_mix)
    z = h @ w_in
    (c_b, c_c, c_x, a_q, a_k, a_v, l_q, l_k, l_v, l_r, l_a,
     gate_a, gate_b, gate_c) = jnp.split(z, _split_points(), axis=-1)
    y_a, conv_new = short_conv(c_b, c_c, c_x, conv_buf, conv_w)
    q = rmsnorm(a_q.reshape(n, t, ATT_HEADS, ATT_HEAD_DIM), g_q)
    k = rmsnorm(a_k.reshape(n, t, ATT_HEADS, ATT_HEAD_DIM), g_k)
    v = a_v.reshape(n, t, ATT_HEADS, ATT_HEAD_DIM)
    if is_prompt:
        y_b = band_attention_prompt(q, k, v, rel_bias)
        keep = min(WINDOW, t)
        k_new, v_new = k[:, t - keep:], v[:, t - keep:]
    else:
        y_b = band_attention_sample(q, k, v, k_cache, v_cache, rel_bias)
        k_new, v_new = k, v
    log_a = jax.nn.log_sigmoid((l_a @ w_a2 + b_a).astype(jnp.float32)) / GLA_TAU
    o, gla_new = gla_recurrence(
        (l_q * (GLA_DK ** -0.5)).reshape(n, t, GLA_HEADS, GLA_DK),
        l_k.reshape(n, t, GLA_HEADS, GLA_DK),
        l_v.reshape(n, t, GLA_HEADS, GLA_DV),
        log_a.reshape(n, t, GLA_HEADS, GLA_DK),
        gla_s0)
    y_c = rmsnorm(o, g_gla).reshape(n, t, GLA_VDIM) * jax.nn.silu(l_r)
    wb_a, wb_b, wb_c = jnp.split(w_branch, [CONV_DIM, CONV_DIM + ATT_DIM], axis=0)
    m = (jax.nn.sigmoid(gate_a) * (y_a @ wb_a)
         + jax.nn.sigmoid(gate_b) * (y_b @ wb_b)
         + jax.nn.sigmoid(gate_c) * (y_c @ wb_c))
    x = x + m @ w_out
    u_g, u_u = jnp.split(rmsnorm(x, g_ffn) @ w_gu, 2, axis=-1)
    x = x + (jax.nn.silu(u_g) * u_u) @ w_down
    return x, conv_new, k_new, v_new, gla_new


def setup_inputs(seed: int = 0) -> dict:
    key = jax.random.key(seed)
    ks = jax.random.split(key, 24)

    def nrm(k, shape, scale):
        return jax.random.normal(k, shape, jnp.float32) * scale

    def gain(k, shape):
        return 1.0 + 0.01 * jax.random.normal(k, shape, jnp.float32)

    win_rows = min(WINDOW, PAST_LEN)
    return {
        'x_prompt': nrm(ks[0], (BATCH, SEQ, D_MODEL), 1.0),
        'x_sample': nrm(ks[1], (DEC_BATCH, DEC_SEQ, D_MODEL), 1.0),
        'cache_conv': nrm(ks[2], (DEPTH, DEC_BATCH, CONV_WIDTH - 1, CONV_DIM), 1.0),
        'cache_k': nrm(ks[3], (DEPTH, DEC_BATCH, win_rows, ATT_HEADS, ATT_HEAD_DIM), 1.0),
        'cache_v': nrm(ks[4], (DEPTH, DEC_BATCH, win_rows, ATT_HEADS, ATT_HEAD_DIM), 1.0),
        'state_gla': nrm(ks[5], (DEPTH, DEC_BATCH, GLA_HEADS, GLA_DK, GLA_DV), 1.0),
        'g_mix': gain(ks[6], (DEPTH, D_MODEL)),
        'w_in': nrm(ks[7], (DEPTH, D_MODEL, IN_DIM), D_MODEL ** -0.5),
        'conv_w': nrm(ks[8], (DEPTH, CONV_WIDTH, CONV_DIM), CONV_WIDTH ** -0.5),
        'g_q': gain(ks[9], (DEPTH, ATT_HEAD_DIM)),
        'g_k': gain(ks[10], (DEPTH, ATT_HEAD_DIM)),
        'rel_bias': nrm(ks[11], (DEPTH, ATT_HEADS, 2 * REL_CLIP + 1), 0.1),
        'w_a2': nrm(ks[12], (DEPTH, GLA_RANK, GLA_KDIM), GLA_RANK ** -0.5),
        'b_a': nrm(ks[13], (DEPTH, GLA_KDIM), 0.1),
        'g_gla': gain(ks[14], (DEPTH, GLA_DV)),
        'w_branch': nrm(ks[15], (DEPTH, MIX_DIM, D_MODEL), CONV_DIM ** -0.5),
        'w_out': nrm(ks[16], (DEPTH, D_MODEL, D_MODEL), D_MODEL ** -0.5),
        'g_ffn': gain(ks[17], (DEPTH, D_MODEL)),
        'w_gu': nrm(ks[18], (DEPTH, D_MODEL, 2 * D_FF), D_MODEL ** -0.5),
        'w_down': nrm(ks[19], (DEPTH, D_FF, D_MODEL), D_FF ** -0.5),
    }


def reference(x_prompt, x_sample, cache_conv, cache_k, cache_v, state_gla, g_mix, w_in, conv_w,
              g_q, g_k, rel_bias, w_a2, b_a, g_gla, w_branch, w_out, g_ffn, w_gu, w_down):
    xp, xs = x_prompt, x_sample
    nb = xp.shape[0]
    conv_p, k_p, v_p, gla_p = [], [], [], []
    conv_s, k_s, v_s, gla_s = [], [], [], []
    for l in range(DEPTH):
        params = (g_mix[l], w_in[l], conv_w[l], g_q[l], g_k[l], rel_bias[l], w_a2[l], b_a[l],
                  g_gla[l], w_branch[l], w_out[l], g_ffn[l], w_gu[l], w_down[l])
        xp, c_new, k_new, v_new, s_new = trunk_layer(
            xp, jnp.zeros((nb, CONV_WIDTH - 1, CONV_DIM), xp.dtype), None, None,
            jnp.zeros((nb, GLA_HEADS, GLA_DK, GLA_DV), xp.dtype), *params, is_prompt=True)
        conv_p.append(c_new); k_p.append(k_new); v_p.append(v_new); gla_p.append(s_new)
        xs, c_new, k_new, v_new, s_new = trunk_layer(
            xs, cache_conv[l], cache_k[l], cache_v[l], state_gla[l], *params, is_prompt=False)
        conv_s.append(c_new); k_s.append(k_new); v_s.append(v_new); gla_s.append(s_new)
    return (xp, xs,
            jnp.stack(conv_p), jnp.stack(k_p), jnp.stack(v_p), jnp.stack(gla_p),
            jnp.stack(conv_s), jnp.stack(k_s), jnp.stack(v_s), jnp.stack(gla_s))
```

```python
import functools

import jax
import jax.numpy as jnp
from jax import lax
from jax.experimental import pallas as pl
from jax.experimental.pallas import tpu as pltpu

F32 = jnp.float32
BF16 = jnp.bfloat16
HIGHEST = lax.Precision.HIGHEST

CHUNK = 64
BAND_CHUNKS = 8
WINDOW = BAND_CHUNKS * CHUNK
SPAN = WINDOW + CHUNK
REL_CLIP = 128
ATT_HEADS = 8
ATT_HEAD_DIM = 128
GLA_HEADS = 4
GLA_DK = 256
GLA_DV = 512
GLA_RANK = 16
GLA_TAU = 16.0
GLA_CHUNK = 32
LA_PAD = 128
NEG_INF = -1e30
EPS = 1e-6
VMEM_LIMIT = 56 * 1024 * 1024

NT_DIMS = (((1,), (1,)), ((), ()))
TN_DIMS = (((0,), (0,)), ((), ()))


def _params(sem, vmem=None):
    return pltpu.CompilerParams(dimension_semantics=sem, vmem_limit_bytes=vmem)


def _rms(x, g):
    return x * lax.rsqrt(jnp.mean(x * x, axis=-1, keepdims=True) + EPS) * g


def _norm_body(x_ref, g_ref, h_ref):
    h_ref[...] = _rms(x_ref[...], g_ref[...]).astype(h_ref.dtype)


def _norm_la_body(x_ref, g_ref, wla_ref, h_ref, la_ref):
    h = _rms(x_ref[...], g_ref[...]).astype(h_ref.dtype)
    h_ref[...] = h
    la_ref[...] = jnp.dot(h, wla_ref[...], preferred_element_type=F32)


def rmsnorm_bf16(x, g, wla=None, tr=512):
    m, d = x.shape
    g2 = g.reshape(1, d)
    x_spec = pl.BlockSpec((tr, d), lambda i: (i, 0))
    g_spec = pl.BlockSpec((1, d), lambda i: (0, 0))
    if wla is None:
        return pl.pallas_call(
            _norm_body, grid=(m // tr,),
            in_specs=[x_spec, g_spec], out_specs=x_spec,
            out_shape=jax.ShapeDtypeStruct((m, d), BF16),
            compiler_params=_params(("parallel",), VMEM_LIMIT), name="rmsnorm",
        )(x, g2)
    return pl.pallas_call(
        _norm_la_body, grid=(m // tr,),
        in_specs=[x_spec, g_spec, pl.BlockSpec((d, LA_PAD), lambda i: (0, 0))],
        out_specs=[x_spec, pl.BlockSpec((tr, LA_PAD), lambda i: (i, 0))],
        out_shape=[jax.ShapeDtypeStruct((m, d), BF16), jax.ShapeDtypeStruct((m, LA_PAD), F32)],
        compiler_params=_params(("parallel",), VMEM_LIMIT), name="rmsnorm_la",
    )(x, g2, wla)


def _mm_body(a_ref, b_ref, o_ref):
    o_ref[...] = jnp.dot(a_ref[...], b_ref[...], preferred_element_type=F32).astype(o_ref.dtype)


def matmul(a, b, out_dtype, tm=1024, tn=1024):
    m, k = a.shape
    n = b.shape[1]
    return pl.pallas_call(
        _mm_body, grid=(m // tm, n // tn),
        in_specs=[pl.BlockSpec((tm, k), lambda i, j: (i, 0)),
                  pl.BlockSpec((k, tn), lambda i, j: (0, j))],
        out_specs=pl.BlockSpec((tm, tn), lambda i, j: (i, j)),
        out_shape=jax.ShapeDtypeStruct((m, n), out_dtype),
        compiler_params=_params(("parallel", "parallel"), VMEM_LIMIT), name="matmul",
    )(a, b)


def _mm_res_body(a_ref, b_ref, r_ref, o_ref):
    o_ref[...] = r_ref[...] + jnp.dot(a_ref[...], b_ref[...], preferred_element_type=F32)


def matmul_residual(a, b, res, tm=1024, tn=512):
    m, k = a.shape
    n = b.shape[1]
    return pl.pallas_call(
        _mm_res_body, grid=(m // tm, n // tn),
        in_specs=[pl.BlockSpec((tm, k), lambda i, j: (i, 0)),
                  pl.BlockSpec((k, tn), lambda i, j: (0, j)),
                  pl.BlockSpec((tm, tn), lambda i, j: (i, j))],
        out_specs=pl.BlockSpec((tm, tn), lambda i, j: (i, j)),
        out_shape=jax.ShapeDtypeStruct((m, n), F32),
        compiler_params=_params(("parallel", "parallel"), VMEM_LIMIT), name="matmul_residual",
    )(a, b, res)


def _swiglu_body(a_ref, wg_ref, wu_ref, o_ref):
    a = a_ref[...]
    ug = jnp.dot(a, wg_ref[...], preferred_element_type=F32)
    uu = jnp.dot(a, wu_ref[...], preferred_element_type=F32)
    o_ref[...] = (ug * jax.nn.sigmoid(ug) * uu).astype(o_ref.dtype)


def swiglu_up(a, w_gu, tm=1024, tn=256):
    m, k = a.shape
    dff = w_gu.shape[1] // 2
    nb = dff // tn
    return pl.pallas_call(
        _swiglu_body, grid=(m // tm, nb),
        in_specs=[pl.BlockSpec((tm, k), lambda i, j: (i, 0)),
                  pl.BlockSpec((k, tn), lambda i, j: (0, j)),
                  pl.BlockSpec((k, tn), lambda i, j: (0, j + nb))],
        out_specs=pl.BlockSpec((tm, tn), lambda i, j: (i, j)),
        out_shape=jax.ShapeDtypeStruct((m, dff), BF16),
        compiler_params=_params(("parallel", "parallel"), VMEM_LIMIT), name="swiglu_up",
    )(a, w_gu, w_gu)


def _mm_res_k_body(a_ref, b_ref, r_ref, o_ref, acc_ref):
    kk = pl.program_id(2)

    @pl.when(kk == 0)
    def _():
        acc_ref[...] = r_ref[...]

    acc_ref[...] += jnp.dot(a_ref[...], b_ref[...], preferred_element_type=F32)

    @pl.when(kk == pl.num_programs(2) - 1)
    def _():
        o_ref[...] = acc_ref[...]


def matmul_residual_ktiled(a, b, res, tm=1024, tn=512, nk=2):
    m, k = a.shape
    n = b.shape[1]
    tk = k // nk
    return pl.pallas_call(
        _mm_res_k_body, grid=(m // tm, n // tn, nk),
        in_specs=[pl.BlockSpec((tm, tk), lambda i, j, kk: (i, kk)),
                  pl.BlockSpec((tk, tn), lambda i, j, kk: (kk, j)),
                  pl.BlockSpec((tm, tn), lambda i, j, kk: (i, j))],
        out_specs=pl.BlockSpec((tm, tn), lambda i, j, kk: (i, j)),
        out_shape=jax.ShapeDtypeStruct((m, n), F32),
        scratch_shapes=[pltpu.VMEM((tm, tn), F32)],
        compiler_params=_params(("parallel", "parallel", "arbitrary"), VMEM_LIMIT),
        name="matmul_residual_ktiled",
    )(a, b, res)


def _conv_body(n_prompt_groups, cb_ref, cc_ref, cx_ref, pc_ref, px_ref, cache_ref, w_ref,
               y_ref, tail_ref):
    g = pl.program_id(0)
    u = cc_ref[...] * cx_ref[...]
    rows = u.shape[0]
    prev = jnp.where(g > 0, pc_ref[...] * px_ref[...], 0.0)
    is_prompt = g < n_prompt_groups
    um2 = jnp.where(is_prompt, prev[6:7], cache_ref[0, 0:1])
    um1 = jnp.where(is_prompt, prev[7:8], cache_ref[0, 1:2])
    row = lax.broadcasted_iota(jnp.int32, u.shape, 0)
    u1 = jnp.where(row == 0, um1, pltpu.roll(u, 1, 0))
    u2 = jnp.where(row == 0, um2, jnp.where(row == 1, um1, pltpu.roll(u, 2, 0)))
    w = w_ref[...]
    y = u2 * w[0:1] + u1 * w[1:2] + u * w[2:3]
    y_ref[...] = (cb_ref[...] * y).astype(y_ref.dtype)
    tail_ref[0] = u[rows - 8:rows]


def gated_conv(z, cache, w, n_prompt_rows, group=CHUNK):
    m = z.shape[0]
    c = w.shape[1]
    ng = m // group
    npg = n_prompt_rows // group
    sub = group // 8
    row_spec = lambda col: pl.BlockSpec((group, c), lambda g: (g, col))
    prev_spec = lambda col: pl.BlockSpec((8, c), lambda g: (jnp.maximum(g * sub - 1, 0), col))
    return pl.pallas_call(
        functools.partial(_conv_body, npg), grid=(ng,),
        in_specs=[row_spec(0), row_spec(1), row_spec(2), prev_spec(1), prev_spec(2),
                  pl.BlockSpec((1, 2, c), lambda g: (jnp.maximum(g - npg, 0), 0, 0)),
                  pl.BlockSpec((3, c), lambda g: (0, 0))],
        out_specs=[pl.BlockSpec((group, c), lambda g: (g, 0)),
                   pl.BlockSpec((1, 8, c), lambda g: (g, 0, 0))],
        out_shape=[jax.ShapeDtypeStruct((m, c), BF16), jax.ShapeDtypeStruct((ng, 8, c), F32)],
        compiler_params=_params(("parallel",)), name="gated_conv",
    )(z, z, z, z, z, cache, w)


def _qk_body(heads, aq_ref, ak_ref, av_ref, gq_ref, gk_ref, qn_ref, kn_ref, knb_ref, vb_ref):
    dh = gq_ref.shape[1]
    gq = gq_ref[...]
    gk = gk_ref[...]
    for h in range(heads):
        sl = slice(h * dh, (h + 1) * dh)
        qn_ref[:, sl] = _rms(aq_ref[:, sl], gq).astype(qn_ref.dtype)
        kn = _rms(ak_ref[:, sl], gk)
        kn_ref[:, sl] = kn
        knb_ref[:, sl] = kn.astype(knb_ref.dtype)
    vb_ref[...] = av_ref[...].astype(vb_ref.dtype)


def qk_norm(z, g_q, g_k, col0, heads, tr=512):
    m = z.shape[0]
    dh = g_q.shape[0]
    c = heads * dh
    spec = lambda col: pl.BlockSpec((tr, c), lambda i: (i, col))
    g_spec = pl.BlockSpec((1, dh), lambda i: (0, 0))
    o_spec = pl.BlockSpec((tr, c), lambda i: (i, 0))
    return pl.pallas_call(
        functools.partial(_qk_body, heads), grid=(m // tr,),
        in_specs=[spec(col0), spec(col0 + 1), spec(col0 + 2), g_spec, g_spec],
        out_specs=[o_spec, o_spec, o_spec, o_spec],
        out_shape=[jax.ShapeDtypeStruct((m, c), BF16), jax.ShapeDtypeStruct((m, c), F32),
                   jax.ShapeDtypeStruct((m, c), BF16), jax.ShapeDtypeStruct((m, c), BF16)],
        compiler_params=_params(("parallel",)), name="qk_norm",
    )(z, z, z, g_q.reshape(1, dh), g_k.reshape(1, dh))


def _attn_body(n_prompt_tiles, chunks_per_tile, prompt_key_rows, scale,
               q_ref, k_ref, v_ref, bias_ref, o_ref):
    t = pl.program_id(1)
    bias = bias_ref[0]
    col = lax.broadcasted_iota(jnp.int32, (CHUNK, SPAN), 1)
    is_prompt = t < n_prompt_tiles
    for c in range(chunks_per_tile):
        g = t * chunks_per_tile + c
        n = g - n_prompt_tiles * chunks_per_tile
        start = pl.multiple_of(jnp.where(is_prompt, g * CHUNK, prompt_key_rows + n * SPAN), CHUNK)
        first_valid = jnp.where(is_prompt, WINDOW - g * CHUNK, 0)
        q = q_ref[c * CHUNK:(c + 1) * CHUNK, :]
        k = k_ref[pl.ds(start, SPAN), :]
        v = v_ref[pl.ds(start, SPAN), :]
        s = lax.dot_general(q, k, NT_DIMS, preferred_element_type=F32) * scale + bias
        s = jnp.where(col >= first_valid, s, NEG_INF)
        p = jnp.exp(s - jnp.max(s, axis=-1, keepdims=True))
        p = p / jnp.sum(p, axis=-1, keepdims=True)
        o = jnp.dot(p.astype(v.dtype), v, preferred_element_type=F32)
        o_ref[c * CHUNK:(c + 1) * CHUNK, :] = o.astype(o_ref.dtype)


def band_attention(qn, keys, vals, bias, n_prompt_rows, chunks_per_tile=8):
    m, c = qn.shape
    heads = bias.shape[0]
    dh = c // heads
    tr = CHUNK * chunks_per_tile
    kr = keys.shape[0]
    body = functools.partial(_attn_body, n_prompt_rows // tr, chunks_per_tile,
                             n_prompt_rows + WINDOW, dh ** -0.5)
    return pl.pallas_call(
        body, grid=(heads, m // tr),
        in_specs=[pl.BlockSpec((tr, dh), lambda h, t: (t, h)),
                  pl.BlockSpec((kr, dh), lambda h, t: (0, h)),
                  pl.BlockSpec((kr, dh), lambda h, t: (0, h)),
                  pl.BlockSpec((1, CHUNK, SPAN), lambda h, t: (h, 0, 0))],
        out_specs=pl.BlockSpec((tr, dh), lambda h, t: (t, h)),
        out_shape=jax.ShapeDtypeStruct((m, c), BF16),
        compiler_params=_params(("parallel", "parallel"), VMEM_LIMIT), name="band_attention",
    )(qn, keys, vals, bias)


def rel_bias_table(rel_bias):
    rel = jnp.clip(jnp.arange(CHUNK)[:, None] - jnp.arange(SPAN)[None, :] + WINDOW,
                   -REL_CLIP, REL_CLIP) + REL_CLIP
    return rel_bias.astype(F32)[:, rel]


def _gla_body(n_prompt_tiles, lq_ref, lk_ref, lv_ref, lr_ref, la_ref, wa2_ref, ba_ref, g_ref,
              s0_ref, y_ref, sout_ref, state_ref):
    t = pl.program_id(1)
    rows, dk = lq_ref.shape
    dv = lv_ref.shape[1]
    cb = GLA_CHUNK
    seq_start = jnp.logical_or(t == 0, t >= n_prompt_tiles)
    seq_end = t >= n_prompt_tiles - 1

    @pl.when(seq_start)
    def _():
        state_ref[...] = s0_ref[0, 0]

    x = jnp.dot(la_ref[...], wa2_ref[...], precision=HIGHEST, preferred_element_type=F32) + ba_ref[...]
    log_a = (jnp.minimum(x, 0.0) - jnp.log1p(jnp.exp(-jnp.abs(x)))) * (1.0 / GLA_TAU)

    r_i = lax.broadcasted_iota(jnp.int32, (cb, cb), 0)
    c_i = lax.broadcasted_iota(jnp.int32, (cb, cb), 1)
    causal = r_i >= c_i
    tri = causal.astype(F32)
    ones = jnp.ones((cb, 128), F32)
    g = g_ref[...]
    for c in range(rows // cb):
        sl = slice(c * cb, (c + 1) * cb)
        a = log_a[sl]
        b = jnp.dot(tri, a, precision=HIGHEST, preferred_element_type=F32)
        b_mid = b[cb // 2 - 1:cb // 2]
        b_last = b[cb - 1:cb]
        q = lq_ref[sl, :] * (dk ** -0.5)
        k = lk_ref[sl, :]
        v = lv_ref[sl, :].astype(BF16)
        q_dec = (q * jnp.exp(b)).astype(BF16)
        q_mid = (q * jnp.exp(b - b_mid)).astype(BF16)
        k_mid = (k * jnp.exp(b_mid - b)).astype(BF16)
        k_rem = (k * jnp.exp(b_last - b)).astype(BF16)
        state = state_ref[...]
        o_inter = jnp.dot(q_dec, state.astype(BF16), preferred_element_type=F32)
        att = lax.dot_general(q_mid, k_mid, NT_DIMS, preferred_element_type=F32)
        att = jnp.where(causal, att, 0.0)
        o = o_inter + jnp.dot(att.astype(BF16), v, preferred_element_type=F32)
        decay = jnp.exp(lax.dot_general(a, ones, TN_DIMS, precision=HIGHEST, preferred_element_type=F32))
        kv = lax.dot_general(k_rem, v, TN_DIMS, preferred_element_type=F32)
        state_ref[...] = jnp.concatenate([decay] * (dv // 128), axis=1) * state + kv
        r = lr_ref[sl, :]
        y_ref[sl, :] = (_rms(o, g) * (r * jax.nn.sigmoid(r))).astype(y_ref.dtype)

    @pl.when(seq_end)
    def _():
        sout_ref[0, 0] = state_ref[...]


def gla(z, la, w_a2, b_a, g_gla, s0, n_prompt_rows, col_q, col_k, col_v, col_r, tr=64):
    m = z.shape[0]
    nseq, heads, dk, dv = s0.shape
    npt = n_prompt_rows // tr
    seq_of = lambda t: jnp.maximum(t - (npt - 1), 0)
    wa2p = jnp.zeros((LA_PAD, heads * dk), F32).at[:w_a2.shape[0]].set(w_a2)
    return pl.pallas_call(
        functools.partial(_gla_body, npt), grid=(heads, m // tr),
        in_specs=[pl.BlockSpec((tr, dk), lambda h, t: (t, col_q // dk + h)),
                  pl.BlockSpec((tr, dk), lambda h, t: (t, col_k // dk + h)),
                  pl.BlockSpec((tr, dv), lambda h, t: (t, col_v // dv + h)),
                  pl.BlockSpec((tr, dv), lambda h, t: (t, col_r // dv + h)),
                  pl.BlockSpec((tr, LA_PAD), lambda h, t: (t, 0)),
                  pl.BlockSpec((LA_PAD, dk), lambda h, t: (0, h)),
                  pl.BlockSpec((1, dk), lambda h, t: (0, h)),
                  pl.BlockSpec((1, dv), lambda h, t: (0, 0)),
                  pl.BlockSpec((1, 1, dk, dv), lambda h, t: (seq_of(t), h, 0, 0))],
        out_specs=[pl.BlockSpec((tr, dv), lambda h, t: (t, h)),
                   pl.BlockSpec((1, 1, dk, dv), lambda h, t: (seq_of(t), h, 0, 0))],
        out_shape=[jax.ShapeDtypeStruct((m, heads * dv), BF16),
                   jax.ShapeDtypeStruct((nseq, heads, dk, dv), F32)],
        scratch_shapes=[pltpu.VMEM((dk, dv), F32)],
        compiler_params=_params(("parallel", "arbitrary"), VMEM_LIMIT), name="gla",
    )(z, z, z, z, la, wa2p, b_a.reshape(1, -1), g_gla.reshape(1, -1), s0)


def _branch_body(ya_ref, yb_ref, yc_ref, wa_ref, wb_ref, wc_ref, ga_ref, gb_ref, gc_ref, o_ref):
    m = jax.nn.sigmoid(ga_ref[...]) * jnp.dot(ya_ref[...], wa_ref[...], preferred_element_type=F32)
    m = m + jax.nn.sigmoid(gb_ref[...]) * jnp.dot(yb_ref[...], wb_ref[...], preferred_element_type=F32)
    m = m + jax.nn.sigmoid(gc_ref[...]) * jnp.dot(yc_ref[...], wc_ref[...], preferred_element_type=F32)
    o_ref[...] = m.astype(o_ref.dtype)


def branch_merge(ya, yb, yc, w_branch, z, gate_col, tm=512, tn=1024):
    m = ya.shape[0]
    ca, cb, cc = ya.shape[1], yb.shape[1], yc.shape[1]
    d = w_branch.shape[1]
    assert ca == cb and cc % ca == 0 and (ca + cb) % cc == 0
    gb = gate_col // tn
    nd = d // tn
    return pl.pallas_call(
        _branch_body, grid=(m // tm, nd),
        in_specs=[pl.BlockSpec((tm, ca), lambda i, j: (i, 0)),
                  pl.BlockSpec((tm, cb), lambda i, j: (i, 0)),
                  pl.BlockSpec((tm, cc), lambda i, j: (i, 0)),
                  pl.BlockSpec((ca, tn), lambda i, j: (0, j)),
                  pl.BlockSpec((cb, tn), lambda i, j: (1, j)),
                  pl.BlockSpec((cc, tn), lambda i, j: ((ca + cb) // cc, j)),
                  pl.BlockSpec((tm, tn), lambda i, j: (i, gb + j)),
                  pl.BlockSpec((tm, tn), lambda i, j: (i, gb + nd + j)),
                  pl.BlockSpec((tm, tn), lambda i, j: (i, gb + 2 * nd + j))],
        out_specs=pl.BlockSpec((tm, tn), lambda i, j: (i, j)),
        out_shape=jax.ShapeDtypeStruct((m, d), BF16),
        compiler_params=_params(("parallel", "parallel"), VMEM_LIMIT), name="branch_merge",
    )(ya, yb, yc, w_branch, w_branch, w_branch, z, z, z)


def kernel(x_prompt, x_sample, cache_conv, cache_k, cache_v, state_gla, g_mix, w_in, conv_w, g_q, g_k,
           rel_bias, w_a2, b_a, g_gla, w_branch, w_out, g_ffn, w_gu, w_down):
    nb, seq, d = x_prompt.shape
    nd, ds, _ = x_sample.shape
    depth = g_mix.shape[0]
    assert nb == 1
    mp = nb * seq
    ms = nd * ds
    conv_dim = conv_w.shape[2]
    att_dim = ATT_HEADS * ATT_HEAD_DIM
    kdim = GLA_HEADS * GLA_DK
    vdim = GLA_HEADS * GLA_DV
    col_att = 3 * conv_dim
    col_lq = col_att + 3 * att_dim
    col_lk = col_lq + kdim
    col_lv = col_lk + kdim
    col_lr = col_lv + vdim
    col_la = col_lr + vdim
    col_gate = col_la
    keep = min(WINDOW, seq)

    x = jnp.concatenate([x_prompt.reshape(mp, d), x_sample.reshape(ms, d)], axis=0)
    outs = {name: [] for name in ("conv_p", "k_p", "v_p", "gla_p", "conv_s", "k_s", "v_s", "gla_s")}
    for l in range(depth):
        w_main = jnp.concatenate([w_in[l, :, :col_la], w_in[l, :, col_la + GLA_RANK:]], axis=1).astype(BF16)
        w_la = jnp.pad(w_in[l, :, col_la:col_la + GLA_RANK], ((0, 0), (0, LA_PAD - GLA_RANK))).astype(BF16)

        h, la = rmsnorm_bf16(x, g_mix[l], w_la)
        z = matmul(h, w_main, F32)

        y_a, conv_tail = gated_conv(z, cache_conv[l], conv_w[l], mp)

        qn, kn, knb, vb = qk_norm(z, g_q[l], g_k[l], col_att // att_dim, ATT_HEADS)
        zpad = jnp.zeros((WINDOW, att_dim), BF16)
        keys = jnp.concatenate(
            [zpad, knb[:mp],
             jnp.concatenate([cache_k[l].reshape(nd, WINDOW, att_dim).astype(BF16),
                              knb[mp:].reshape(nd, ds, att_dim)], axis=1).reshape(nd * SPAN, att_dim)], axis=0)
        vals = jnp.concatenate(
            [zpad, vb[:mp],
             jnp.concatenate([cache_v[l].reshape(nd, WINDOW, att_dim).astype(BF16),
                              vb[mp:].reshape(nd, ds, att_dim)], axis=1).reshape(nd * SPAN, att_dim)], axis=0)
        y_b = band_attention(qn, keys, vals, rel_bias_table(rel_bias[l]), mp)

        s0 = jnp.concatenate([jnp.zeros((nb,) + state_gla.shape[2:], F32), state_gla[l]], axis=0)
        y_c, s_new = gla(z, la, w_a2[l], b_a[l], g_gla[l], s0, mp, col_lq, col_lk, col_lv, col_lr)

        mrg = branch_merge(y_a, y_b, y_c, w_branch[l].astype(BF16), z, col_gate)
        x = matmul_residual(mrg, w_out[l].astype(BF16), x)

        hf = rmsnorm_bf16(x, g_ffn[l])
        hid = swiglu_up(hf, w_gu[l].astype(BF16))
        x = matmul_residual_ktiled(hid, w_down[l].astype(BF16), x)

        v_new = z[:, col_att + 2 * att_dim:col_att + 3 * att_dim]
        outs["conv_p"].append(conv_tail[mp // CHUNK - 1, 6:8][None])
        outs["conv_s"].append(conv_tail[mp // CHUNK:, 6:8])
        outs["k_p"].append(kn[mp - keep:mp].reshape(nb, keep, ATT_HEADS, ATT_HEAD_DIM))
        outs["v_p"].append(v_new[mp - keep:mp].reshape(nb, keep, ATT_HEADS, ATT_HEAD_DIM))
        outs["k_s"].append(kn[mp:].reshape(nd, ds, ATT_HEADS, ATT_HEAD_DIM))
        outs["v_s"].append(v_new[mp:].reshape(nd, ds, ATT_HEADS, ATT_HEAD_DIM))
        outs["gla_p"].append(s_new[:nb])
        outs["gla_s"].append(s_new[nb:])

    st = {name: jnp.stack(v) for name, v in outs.items()}
    return (x[:mp].reshape(nb, seq, d), x[mp:].reshape(nd, ds, d),
            st["conv_p"], st["k_p"], st["v_p"], st["gla_p"],
            st["conv_s"], st["k_s"], st["v_s"], st["gla_s"])
```

```python
import functools

import jax
import jax.numpy as jnp
from jax import lax
from jax.experimental import pallas as pl
from jax.experimental.pallas import tpu as pltpu

F32 = jnp.float32
BF16 = jnp.bfloat16
HIGHEST = lax.Precision.HIGHEST

CHUNK = 64
BAND_CHUNKS = 8
WINDOW = BAND_CHUNKS * CHUNK
SPAN = WINDOW + CHUNK
BIAS_LANES = SPAN + CHUNK
REL_CLIP = 128
ATT_HEADS = 8
ATT_HEAD_DIM = 128
GLA_HEADS = 4
GLA_DK = 256
GLA_DV = 512
GLA_RANK = 16
GLA_TAU = 16.0
GLA_CHUNK = 32
LA_PAD = 128
NEG_INF = -1e30
EPS = 1e-6
VMEM_LIMIT = 56 * 1024 * 1024

NT_DIMS = (((1,), (1,)), ((), ()))
TN_DIMS = (((0,), (0,)), ((), ()))


def _params(sem, vmem=None):
    return pltpu.CompilerParams(dimension_semantics=sem, vmem_limit_bytes=vmem)


def _rms(x, g):
    return x * lax.rsqrt(jnp.mean(x * x, axis=-1, keepdims=True) + EPS) * g


def _norm_body(x_ref, g_ref, h_ref):
    h_ref[...] = _rms(x_ref[...], g_ref[...]).astype(h_ref.dtype)


def _norm_la_body(x_ref, g_ref, wla_ref, h_ref, la_ref):
    h = _rms(x_ref[...], g_ref[...]).astype(h_ref.dtype)
    h_ref[...] = h
    la_ref[...] = jnp.dot(h, wla_ref[...], preferred_element_type=F32)


def rmsnorm_bf16(x, g, wla=None, tr=512):
    m, d = x.shape
    g2 = g.reshape(1, d)
    x_spec = pl.BlockSpec((tr, d), lambda i: (i, 0))
    g_spec = pl.BlockSpec((1, d), lambda i: (0, 0))
    if wla is None:
        return pl.pallas_call(
            _norm_body, grid=(m // tr,),
            in_specs=[x_spec, g_spec], out_specs=x_spec,
            out_shape=jax.ShapeDtypeStruct((m, d), BF16),
            compiler_params=_params(("parallel",), VMEM_LIMIT), name="rmsnorm",
        )(x, g2)
    return pl.pallas_call(
        _norm_la_body, grid=(m // tr,),
        in_specs=[x_spec, g_spec, pl.BlockSpec((d, LA_PAD), lambda i: (0, 0))],
        out_specs=[x_spec, pl.BlockSpec((tr, LA_PAD), lambda i: (i, 0))],
        out_shape=[jax.ShapeDtypeStruct((m, d), BF16), jax.ShapeDtypeStruct((m, LA_PAD), F32)],
        compiler_params=_params(("parallel",), VMEM_LIMIT), name="rmsnorm_la",
    )(x, g2, wla)


CAST_ROWS = 512


def _cast_tile(dst_ref, src_ref):
    def step(r, carry):
        rs = pl.ds(pl.multiple_of(r * CAST_ROWS, CAST_ROWS), CAST_ROWS)
        dst_ref[rs, :] = src_ref[rs, :].astype(dst_ref.dtype)
        return carry
    lax.fori_loop(0, src_ref.shape[0] // CAST_ROWS, step, 0)


def _in_proj_body(n_aligned, shift, a_ref, w_ref, wnext_ref, o_ref, wb_ref):
    j = pl.program_id(0)
    i = pl.program_id(1)
    tn = wb_ref.shape[1]
    wide = tn + wnext_ref.shape[1]

    @pl.when(jnp.logical_and(i == 0, j < n_aligned))
    def _():
        _cast_tile(wb_ref, w_ref)

    @pl.when(jnp.logical_and(i == 0, j >= n_aligned))
    def _():
        def step(r, carry):
            rs = pl.ds(pl.multiple_of(r * CAST_ROWS, CAST_ROWS), CAST_ROWS)
            w = jnp.concatenate([w_ref[rs, :], wnext_ref[rs, :]], axis=1)
            wb_ref[rs, :] = pltpu.roll(w, wide - shift, 1)[:, :tn].astype(wb_ref.dtype)
            return carry
        lax.fori_loop(0, w_ref.shape[0] // CAST_ROWS, step, 0)

    o_ref[...] = jnp.dot(a_ref[...], wb_ref[...], preferred_element_type=F32)


def in_proj(a, w_in, layer, skip_col, skip, tm=1024, tn=512):
    m, k = a.shape
    n = w_in.shape[2] - skip
    assert skip_col % tn == 0 and n % tn == 0 and tn % 128 == 0 and skip < 128
    sub = tn // 128
    return pl.pallas_call(
        functools.partial(_in_proj_body, skip_col // tn, skip), grid=(n // tn, m // tm),
        in_specs=[pl.BlockSpec((tm, k), lambda j, i: (i, 0)),
                  pl.BlockSpec((None, k, tn), lambda j, i: (layer, 0, j)),
                  pl.BlockSpec((None, k, 128), lambda j, i: (layer, 0, (j + 1) * sub))],
        out_specs=pl.BlockSpec((tm, tn), lambda j, i: (i, j)),
        out_shape=jax.ShapeDtypeStruct((m, n), F32),
        scratch_shapes=[pltpu.VMEM((k, tn), BF16)],
        compiler_params=_params(("parallel", "arbitrary"), VMEM_LIMIT), name="in_proj",
    )(a, w_in, w_in)


def _mm_res_body(a_ref, b_ref, r_ref, o_ref):
    o_ref[...] = r_ref[...] + jnp.dot(a_ref[...], b_ref[...], preferred_element_type=F32)


def matmul_residual(a, b, layer, res, tm=1024, tn=512):
    m, k = a.shape
    n = b.shape[2]
    return pl.pallas_call(
        _mm_res_body, grid=(m // tm, n // tn),
        in_specs=[pl.BlockSpec((tm, k), lambda i, j: (i, 0)),
                  pl.BlockSpec((None, k, tn), lambda i, j: (layer, 0, j)),
                  pl.BlockSpec((tm, tn), lambda i, j: (i, j))],
        out_specs=pl.BlockSpec((tm, tn), lambda i, j: (i, j)),
        out_shape=jax.ShapeDtypeStruct((m, n), F32),
        compiler_params=_params(("parallel", "parallel"), VMEM_LIMIT), name="matmul_residual",
    )(a, b, res)


def _swiglu_body(a_ref, wg_ref, wu_ref, o_ref, wgb_ref, wub_ref):
    @pl.when(pl.program_id(1) == 0)
    def _():
        _cast_tile(wgb_ref, wg_ref)
        _cast_tile(wub_ref, wu_ref)

    a = a_ref[...]
    ug = jnp.dot(a, wgb_ref[...], preferred_element_type=F32)
    uu = jnp.dot(a, wub_ref[...], preferred_element_type=F32)
    o_ref[...] = (ug * jax.nn.sigmoid(ug) * uu).astype(o_ref.dtype)


def swiglu_up(a, w_gu, layer, tm=1024, tn=256):
    m, k = a.shape
    dff = w_gu.shape[2] // 2
    nb = dff // tn
    return pl.pallas_call(
        _swiglu_body, grid=(nb, m // tm),
        in_specs=[pl.BlockSpec((tm, k), lambda j, i: (i, 0)),
                  pl.BlockSpec((None, k, tn), lambda j, i: (layer, 0, j)),
                  pl.BlockSpec((None, k, tn), lambda j, i: (layer, 0, j + nb))],
        out_specs=pl.BlockSpec((tm, tn), lambda j, i: (i, j)),
        out_shape=jax.ShapeDtypeStruct((m, dff), BF16),
        scratch_shapes=[pltpu.VMEM((k, tn), BF16), pltpu.VMEM((k, tn), BF16)],
        compiler_params=_params(("parallel", "arbitrary"), VMEM_LIMIT), name="swiglu_up",
    )(a, w_gu, w_gu)


def _mm_res_k_body(a_ref, b_ref, r_ref, o_ref, acc_ref):
    kk = pl.program_id(2)

    @pl.when(kk == 0)
    def _():
        acc_ref[...] = r_ref[...]

    acc_ref[...] += jnp.dot(a_ref[...], b_ref[...], preferred_element_type=F32)

    @pl.when(kk == pl.num_programs(2) - 1)
    def _():
        o_ref[...] = acc_ref[...]


def matmul_residual_ktiled(a, b, layer, res, tm=1024, tn=512, nk=2):
    m, k = a.shape
    n = b.shape[2]
    tk = k // nk
    return pl.pallas_call(
        _mm_res_k_body, grid=(m // tm, n // tn, nk),
        in_specs=[pl.BlockSpec((tm, tk), lambda i, j, kk: (i, kk)),
                  pl.BlockSpec((None, tk, tn), lambda i, j, kk: (layer, kk, j)),
                  pl.BlockSpec((tm, tn), lambda i, j, kk: (i, j))],
        out_specs=pl.BlockSpec((tm, tn), lambda i, j, kk: (i, j)),
        out_shape=jax.ShapeDtypeStruct((m, n), F32),
        scratch_shapes=[pltpu.VMEM((tm, tn), F32)],
        compiler_params=_params(("parallel", "parallel", "arbitrary"), VMEM_LIMIT),
        name="matmul_residual_ktiled",
    )(a, b, res)


def _conv_body(n_prompt_groups, cb_ref, cc_ref, cx_ref, pc_ref, px_ref, cache_ref, w_ref,
               y_ref, tail_ref):
    g = pl.program_id(0)
    u = cc_ref[...] * cx_ref[...]
    rows = u.shape[0]
    prev = jnp.where(g > 0, pc_ref[...] * px_ref[...], 0.0)
    is_prompt = g < n_prompt_groups
    um2 = jnp.where(is_prompt, prev[6:7], cache_ref[0, 0:1])
    um1 = jnp.where(is_prompt, prev[7:8], cache_ref[0, 1:2])
    row = lax.broadcasted_iota(jnp.int32, u.shape, 0)
    u1 = jnp.where(row == 0, um1, pltpu.roll(u, 1, 0))
    u2 = jnp.where(row == 0, um2, jnp.where(row == 1, um1, pltpu.roll(u, 2, 0)))
    w = w_ref[...]
    y = u2 * w[0:1] + u1 * w[1:2] + u * w[2:3]
    y_ref[...] = (cb_ref[...] * y).astype(y_ref.dtype)
    tail_ref[0] = u[rows - 8:rows]


def gated_conv(z, cache, w, n_prompt_rows, group=CHUNK):
    m = z.shape[0]
    c = w.shape[1]
    ng = m // group
    npg = n_prompt_rows // group
    sub = group // 8
    row_spec = lambda col: pl.BlockSpec((group, c), lambda g: (g, col))
    prev_spec = lambda col: pl.BlockSpec((8, c), lambda g: (jnp.maximum(g * sub - 1, 0), col))
    return pl.pallas_call(
        functools.partial(_conv_body, npg), grid=(ng,),
        in_specs=[row_spec(0), row_spec(1), row_spec(2), prev_spec(1), prev_spec(2),
                  pl.BlockSpec((1, 2, c), lambda g: (jnp.maximum(g - npg, 0), 0, 0)),
                  pl.BlockSpec((3, c), lambda g: (0, 0))],
        out_specs=[pl.BlockSpec((group, c), lambda g: (g, 0)),
                   pl.BlockSpec((1, 8, c), lambda g: (g, 0, 0))],
        out_shape=[jax.ShapeDtypeStruct((m, c), BF16), jax.ShapeDtypeStruct((ng, 8, c), F32)],
        compiler_params=_params(("parallel",)), name="gated_conv",
    )(z, z, z, z, z, cache, w)


def _qk_body(heads, aq_ref, ak_ref, av_ref, gq_ref, gk_ref, qn_ref, kn_ref, knb_ref, vb_ref):
    dh = gq_ref.shape[1]
    gq = gq_ref[...]
    gk = gk_ref[...]
    for h in range(heads):
        sl = slice(h * dh, (h + 1) * dh)
        qn_ref[:, sl] = _rms(aq_ref[:, sl], gq).astype(qn_ref.dtype)
        kn = _rms(ak_ref[:, sl], gk)
        kn_ref[:, sl] = kn
        knb_ref[:, sl] = kn.astype(knb_ref.dtype)
    vb_ref[...] = av_ref[...].astype(vb_ref.dtype)


def qk_norm(z, g_q, g_k, col0, heads, tr=512):
    m = z.shape[0]
    dh = g_q.shape[0]
    c = heads * dh
    spec = lambda col: pl.BlockSpec((tr, c), lambda i: (i, col))
    g_spec = pl.BlockSpec((1, dh), lambda i: (0, 0))
    o_spec = pl.BlockSpec((tr, c), lambda i: (i, 0))
    return pl.pallas_call(
        functools.partial(_qk_body, heads), grid=(m // tr,),
        in_specs=[spec(col0), spec(col0 + 1), spec(col0 + 2), g_spec, g_spec],
        out_specs=[o_spec, o_spec, o_spec, o_spec],
        out_shape=[jax.ShapeDtypeStruct((m, c), BF16), jax.ShapeDtypeStruct((m, c), F32),
                   jax.ShapeDtypeStruct((m, c), BF16), jax.ShapeDtypeStruct((m, c), BF16)],
        compiler_params=_params(("parallel",)), name="qk_norm",
    )(z, z, z, g_q.reshape(1, dh), g_k.reshape(1, dh))


def _attn_body(n_prompt_tiles, chunks_per_tile, prompt_key_rows, scale,
               q_ref, k_ref, v_ref, bias_ref, o_ref):
    t = pl.program_id(1)
    bias = pltpu.roll(jnp.broadcast_to(bias_ref[0], (CHUNK, BIAS_LANES)), 0, 1,
                      stride=1, stride_axis=0)[:, :SPAN]
    col = lax.broadcasted_iota(jnp.int32, (CHUNK, SPAN), 1)
    is_prompt = t < n_prompt_tiles
    for c in range(chunks_per_tile):
        g = t * chunks_per_tile + c
        n = g - n_prompt_tiles * chunks_per_tile
        start = pl.multiple_of(jnp.where(is_prompt, g * CHUNK, prompt_key_rows + n * SPAN), CHUNK)
        first_valid = jnp.where(is_prompt, WINDOW - g * CHUNK, 0)
        q = q_ref[c * CHUNK:(c + 1) * CHUNK, :]
        k = k_ref[pl.ds(start, SPAN), :]
        v = v_ref[pl.ds(start, SPAN), :]
        s = lax.dot_general(q, k, NT_DIMS, preferred_element_type=F32) * scale + bias
        s = jnp.where(col >= first_valid, s, NEG_INF)
        p = jnp.exp(s - jnp.max(s, axis=-1, keepdims=True))
        p = p / jnp.sum(p, axis=-1, keepdims=True)
        o = jnp.dot(p.astype(v.dtype), v, preferred_element_type=F32)
        o_ref[c * CHUNK:(c + 1) * CHUNK, :] = o.astype(o_ref.dtype)


def band_attention(qn, keys, vals, bias, n_prompt_rows, chunks_per_tile=8):
    m, c = qn.shape
    heads = bias.shape[0]
    dh = c // heads
    tr = CHUNK * chunks_per_tile
    kr = keys.shape[0]
    body = functools.partial(_attn_body, n_prompt_rows // tr, chunks_per_tile,
                             n_prompt_rows + WINDOW, dh ** -0.5)
    return pl.pallas_call(
        body, grid=(heads, m // tr),
        in_specs=[pl.BlockSpec((tr, dh), lambda h, t: (t, h)),
                  pl.BlockSpec((kr, dh), lambda h, t: (0, h)),
                  pl.BlockSpec((kr, dh), lambda h, t: (0, h)),
                  pl.BlockSpec((1, 1, BIAS_LANES), lambda h, t: (h, 0, 0))],
        out_specs=pl.BlockSpec((tr, dh), lambda h, t: (t, h)),
        out_shape=jax.ShapeDtypeStruct((m, c), BF16),
        compiler_params=_params(("parallel", "parallel"), VMEM_LIMIT), name="band_attention",
    )(qn, keys, vals, bias)


def rel_bias_rows(rel_bias):
    i = jnp.arange(BIAS_LANES)
    dist = jnp.where(i < SPAN, -i, BIAS_LANES - i)
    rel = jnp.clip(dist + WINDOW, -REL_CLIP, REL_CLIP) + REL_CLIP
    return rel_bias.astype(F32)[:, rel].reshape(rel_bias.shape[0], 1, BIAS_LANES)


def _gla_body(n_prompt_tiles, seg_rows, lq_ref, lk_ref, lv_ref, lr_ref, la_ref, wa2_ref, ba_ref, g_ref,
              s0_ref, y_ref, soutp_ref, souts_ref, state_ref, o_ref):
    t = pl.program_id(0)
    rows = lq_ref.shape[0]
    heads, dv, dk = state_ref.shape
    cb = GLA_CHUNK
    nc = rows // cb
    chunks_per_seg = seg_rows // cb
    nseg = rows // seg_rows
    is_sample = t >= n_prompt_tiles
    shift = cb.bit_length() - 1

    r_i = lax.broadcasted_iota(jnp.int32, (rows, rows), 0)
    c_i = lax.broadcasted_iota(jnp.int32, (rows, rows), 1)
    same_chunk = jnp.right_shift(r_i, shift) == jnp.right_shift(c_i, shift)
    chunk_tri = jnp.logical_and(same_chunk, r_i >= c_i).astype(F32)
    causal = (lax.broadcasted_iota(jnp.int32, (cb, cb), 0)
              >= lax.broadcasted_iota(jnp.int32, (cb, cb), 1))[None]

    kd = heads * dk
    x = jnp.dot(la_ref[...], wa2_ref[...], precision=HIGHEST, preferred_element_type=F32) + ba_ref[...]
    log_a = (jnp.minimum(x, 0.0) - jnp.log1p(jnp.exp(-jnp.abs(x)))) * (1.0 / GLA_TAU)
    b = jnp.dot(chunk_tri, log_a, precision=HIGHEST, preferred_element_type=F32).reshape(nc, cb, kd)
    b_mid = b[:, cb // 2 - 1:cb // 2]
    b_last = b[:, cb - 1:cb]
    q = (lq_ref[...] * (dk ** -0.5)).reshape(nc, cb, kd)
    k = lk_ref[...].reshape(nc, cb, kd)
    q_dec_all = (q * jnp.exp(b)).astype(BF16)
    q_mid_all = (q * jnp.exp(b - b_mid)).astype(BF16)
    k_mid_all = (k * jnp.exp(b_mid - b)).astype(BF16)
    k_rem_all = (k * jnp.exp(b_last - b)).astype(BF16)
    decay_all = jnp.exp(b_last)
    pre = []
    for h in range(heads):
        ks = slice(h * dk, (h + 1) * dk)
        v = lv_ref[:, h * dv:(h + 1) * dv].astype(BF16).reshape(nc, cb, dv)
        att = jnp.einsum("clk,cmk->clm", q_mid_all[:, :, ks], k_mid_all[:, :, ks],
                         preferred_element_type=F32)
        att = jnp.where(causal, att, 0.0).astype(BF16)
        o_intra = jnp.einsum("clm,cmv->clv", att, v, preferred_element_type=F32)
        pre.append((q_dec_all[:, :, ks], k_rem_all[:, :, ks], v, o_intra, decay_all[:, :, ks]))

    for s in range(nseg):
        @pl.when(is_sample)
        def _():
            for h in range(heads):
                state_ref[h] = s0_ref[s, h].T

        if s == 0:
            @pl.when(t == 0)
            def _():
                state_ref[...] = jnp.zeros_like(state_ref)

        for c in range(s * chunks_per_seg, (s + 1) * chunks_per_seg):
            for h in range(heads):
                q_dec, k_rem, v, o_intra, decay = pre[h]
                state = state_ref[h]
                o = o_intra[c] + lax.dot_general(q_dec[c], state.astype(BF16), NT_DIMS,
                                                 preferred_element_type=F32)
                kv = lax.dot_general(v[c], k_rem[c], TN_DIMS, preferred_element_type=F32)
                state_ref[h] = decay[c] * state + kv
                o_ref[c * cb:(c + 1) * cb, h * dv:(h + 1) * dv] = o

        @pl.when(is_sample)
        def _():
            for h in range(heads):
                souts_ref[s, h] = state_ref[h].T

        if s == nseg - 1:
            @pl.when(t == n_prompt_tiles - 1)
            def _():
                for h in range(heads):
                    soutp_ref[0, h] = state_ref[h].T

    g = g_ref[...]
    for h in range(heads):
        vs = slice(h * dv, (h + 1) * dv)
        r = lr_ref[:, vs]
        y_ref[:, vs] = (_rms(o_ref[:, vs], g) * (r * jax.nn.sigmoid(r))).astype(y_ref.dtype)


def gla(z, la, w_a2, b_a, g_gla, s0, n_prompt_rows, seg_rows, col_q, col_k, col_v, col_r, tr=128):
    m = z.shape[0]
    nseq, heads, dk, dv = s0.shape
    kd, vd = heads * dk, heads * dv
    npt = n_prompt_rows // tr
    spt = tr // seg_rows
    assert n_prompt_rows % tr == 0 and tr % seg_rows == 0 and seg_rows % GLA_CHUNK == 0
    sample_tile = lambda t: (jnp.maximum(t - npt, 0), 0, 0, 0)
    wa2p = jnp.zeros((LA_PAD, kd), F32).at[:w_a2.shape[0]].set(w_a2)
    return pl.pallas_call(
        functools.partial(_gla_body, npt, seg_rows), grid=(m // tr,),
        in_specs=[pl.BlockSpec((tr, kd), lambda t: (t, col_q // kd)),
                  pl.BlockSpec((tr, kd), lambda t: (t, col_k // kd)),
                  pl.BlockSpec((tr, vd), lambda t: (t, col_v // vd)),
                  pl.BlockSpec((tr, vd), lambda t: (t, col_r // vd)),
                  pl.BlockSpec((tr, LA_PAD), lambda t: (t, 0)),
                  pl.BlockSpec((LA_PAD, kd), lambda t: (0, 0)),
                  pl.BlockSpec((1, kd), lambda t: (0, 0)),
                  pl.BlockSpec((1, dv), lambda t: (0, 0)),
                  pl.BlockSpec((spt, heads, dk, dv), sample_tile)],
        out_specs=[pl.BlockSpec((tr, vd), lambda t: (t, 0)),
                   pl.BlockSpec((1, heads, dk, dv), lambda t: (0, 0, 0, 0)),
                   pl.BlockSpec((spt, heads, dk, dv), sample_tile)],
        out_shape=[jax.ShapeDtypeStruct((m, vd), BF16),
                   jax.ShapeDtypeStruct((1, heads, dk, dv), F32),
                   jax.ShapeDtypeStruct((nseq, heads, dk, dv), F32)],
        scratch_shapes=[pltpu.VMEM((heads, dv, dk), F32), pltpu.VMEM((tr, vd), F32)],
        compiler_params=_params(("arbitrary",), VMEM_LIMIT), name="gla",
    )(z, z, z, z, la, wa2p, b_a.reshape(1, -1), g_gla.reshape(1, -1), s0)


def _branch_body(ya_ref, yb_ref, yc_ref, wa_ref, wb_ref, wc_ref, ga_ref, gb_ref, gc_ref, o_ref):
    m = jax.nn.sigmoid(ga_ref[...]) * jnp.dot(ya_ref[...], wa_ref[...], preferred_element_type=F32)
    m = m + jax.nn.sigmoid(gb_ref[...]) * jnp.dot(yb_ref[...], wb_ref[...], preferred_element_type=F32)
    m = m + jax.nn.sigmoid(gc_ref[...]) * jnp.dot(yc_ref[...], wc_ref[...], preferred_element_type=F32)
    o_ref[...] = m.astype(o_ref.dtype)


def branch_merge(ya, yb, yc, w_branch, layer, z, gate_col, tm=512, tn=1024):
    m = ya.shape[0]
    ca, cb, cc = ya.shape[1], yb.shape[1], yc.shape[1]
    d = w_branch.shape[2]
    assert ca == cb and cc % ca == 0 and (ca + cb) % cc == 0
    gb = gate_col // tn
    nd = d // tn
    return pl.pallas_call(
        _branch_body, grid=(m // tm, nd),
        in_specs=[pl.BlockSpec((tm, ca), lambda i, j: (i, 0)),
                  pl.BlockSpec((tm, cb), lambda i, j: (i, 0)),
                  pl.BlockSpec((tm, cc), lambda i, j: (i, 0)),
                  pl.BlockSpec((None, ca, tn), lambda i, j: (layer, 0, j)),
                  pl.BlockSpec((None, cb, tn), lambda i, j: (layer, 1, j)),
                  pl.BlockSpec((None, cc, tn), lambda i, j: (layer, (ca + cb) // cc, j)),
                  pl.BlockSpec((tm, tn), lambda i, j: (i, gb + j)),
                  pl.BlockSpec((tm, tn), lambda i, j: (i, gb + nd + j)),
                  pl.BlockSpec((tm, tn), lambda i, j: (i, gb + 2 * nd + j))],
        out_specs=pl.BlockSpec((tm, tn), lambda i, j: (i, j)),
        out_shape=jax.ShapeDtypeStruct((m, d), BF16),
        compiler_params=_params(("parallel", "parallel"), VMEM_LIMIT), name="branch_merge",
    )(ya, yb, yc, w_branch, w_branch, w_branch, z, z, z)


def kernel(x_prompt, x_sample, cache_conv, cache_k, cache_v, state_gla, g_mix, w_in, conv_w, g_q, g_k,
           rel_bias, w_a2, b_a, g_gla, w_branch, w_out, g_ffn, w_gu, w_down):
    nb, seq, d = x_prompt.shape
    nd, ds, _ = x_sample.shape
    depth = g_mix.shape[0]
    assert nb == 1
    mp = nb * seq
    ms = nd * ds
    conv_dim = conv_w.shape[2]
    att_dim = ATT_HEADS * ATT_HEAD_DIM
    kdim = GLA_HEADS * GLA_DK
    vdim = GLA_HEADS * GLA_DV
    col_att = 3 * conv_dim
    col_lq = col_att + 3 * att_dim
    col_lk = col_lq + kdim
    col_lv = col_lk + kdim
    col_lr = col_lv + vdim
    col_la = col_lr + vdim
    col_gate = col_la
    keep = min(WINDOW, seq)

    x = jnp.concatenate([x_prompt.reshape(mp, d), x_sample.reshape(ms, d)], axis=0)
    w_branch_b = w_branch.astype(BF16)
    w_out_b = w_out.astype(BF16)
    w_down_b = w_down.astype(BF16)
    outs = {name: [] for name in ("conv_p", "k_p", "v_p", "gla_p", "conv_s", "k_s", "v_s", "gla_s")}
    for l in range(depth):
        w_la = jnp.pad(w_in[l, :, col_la:col_la + GLA_RANK], ((0, 0), (0, LA_PAD - GLA_RANK))).astype(BF16)

        h, la = rmsnorm_bf16(x, g_mix[l], w_la)
        z = in_proj(h, w_in, l, col_la, GLA_RANK)

        y_a, conv_tail = gated_conv(z, cache_conv[l], conv_w[l], mp)

        qn, kn, knb, vb = qk_norm(z, g_q[l], g_k[l], col_att // att_dim, ATT_HEADS)
        zpad = jnp.zeros((WINDOW, att_dim), BF16)
        keys = jnp.concatenate(
            [zpad, knb[:mp],
             jnp.concatenate([cache_k[l].reshape(nd, WINDOW, att_dim).astype(BF16),
                              knb[mp:].reshape(nd, ds, att_dim)], axis=1).reshape(nd * SPAN, att_dim)], axis=0)
        vals = jnp.concatenate(
            [zpad, vb[:mp],
             jnp.concatenate([cache_v[l].reshape(nd, WINDOW, att_dim).astype(BF16),
                              vb[mp:].reshape(nd, ds, att_dim)], axis=1).reshape(nd * SPAN, att_dim)], axis=0)
        y_b = band_attention(qn, keys, vals, rel_bias_rows(rel_bias[l]), mp)

        y_c, s_prompt, s_sample = gla(z, la, w_a2[l], b_a[l], g_gla[l], state_gla[l], mp, ds,
                                      col_lq, col_lk, col_lv, col_lr)

        mrg = branch_merge(y_a, y_b, y_c, w_branch_b, l, z, col_gate)
        x = matmul_residual(mrg, w_out_b, l, x)

        hf = rmsnorm_bf16(x, g_ffn[l])
        hid = swiglu_up(hf, w_gu, l)
        x = matmul_residual_ktiled(hid, w_down_b, l, x)

        v_new = z[:, col_att + 2 * att_dim:col_att + 3 * att_dim]
        outs["conv_p"].append(conv_tail[mp // CHUNK - 1, 6:8][None])
        outs["conv_s"].append(conv_tail[mp // CHUNK:, 6:8])
        outs["k_p"].append(kn[mp - keep:mp].reshape(nb, keep, ATT_HEADS, ATT_HEAD_DIM))
        outs["v_p"].append(v_new[mp - keep:mp].reshape(nb, keep, ATT_HEADS, ATT_HEAD_DIM))
        outs["k_s"].append(kn[mp:].reshape(nd, ds, ATT_HEADS, ATT_HEAD_DIM))
        outs["v_s"].append(v_new[mp:].reshape(nd, ds, ATT_HEADS, ATT_HEAD_DIM))
        outs["gla_p"].append(s_prompt)
        outs["gla_s"].append(s_sample)

    st = {name: jnp.stack(v) for name, v in outs.items()}
    return (x[:mp].reshape(nb, seq, d), x[mp:].reshape(nd, ds, d),
            st["conv_p"], st["k_p"], st["v_p"], st["gla_p"],
            st["conv_s"], st["k_s"], st["v_s"], st["gla_s"])
```

```python
import functools

import jax
import jax.numpy as jnp
from jax import lax
from jax.experimental import pallas as pl
from jax.experimental.pallas import tpu as pltpu

F32 = jnp.float32
BF16 = jnp.bfloat16
HIGHEST = lax.Precision.HIGHEST

CHUNK = 64
BAND_CHUNKS = 8
WINDOW = BAND_CHUNKS * CHUNK
SPAN = WINDOW + CHUNK
REL_CLIP = 128
ATT_HEADS = 8
ATT_HEAD_DIM = 128
GLA_HEADS = 4
GLA_DK = 256
GLA_DV = 512
GLA_RANK = 16
GLA_TAU = 16.0
GLA_CHUNK = 32
LA_PAD = 128
PROMPT_ROW_GROUPS = 2
NEG_INF = -1e30
EPS = 1e-6
VMEM_LIMIT = 56 * 1024 * 1024
IN_PROJ_VMEM_LIMIT = 60 * 1024 * 1024

NT_DIMS = (((1,), (1,)), ((), ()))
TN_DIMS = (((0,), (0,)), ((), ()))


def _params(sem, vmem=None):
    return pltpu.CompilerParams(dimension_semantics=sem, vmem_limit_bytes=vmem)


def _rms(x, g):
    return x * lax.rsqrt(jnp.mean(x * x, axis=-1, keepdims=True) + EPS) * g


def _norm_body(x_ref, g_ref, h_ref):
    h_ref[...] = _rms(x_ref[...], g_ref[...]).astype(h_ref.dtype)


def _norm_la_body(x_ref, g_ref, wla_ref, h_ref, la_ref, wlab_ref):
    rank = wla_ref.shape[0]

    @pl.when(pl.program_id(0) == 0)
    def _():
        wlab_ref[...] = jnp.zeros_like(wlab_ref)
        wlab_ref[0:rank, :] = wla_ref[...].astype(wlab_ref.dtype)

    h = _rms(x_ref[...], g_ref[...]).astype(h_ref.dtype)
    h_ref[...] = h
    la_ref[...] = lax.dot_general(h, wlab_ref[...], NT_DIMS, preferred_element_type=F32)


def rmsnorm_bf16(x, g, tr=512):
    m, d = x.shape
    x_spec = pl.BlockSpec((tr, d), lambda i: (i, 0))
    return pl.pallas_call(
        _norm_body, grid=(m // tr,),
        in_specs=[x_spec, pl.BlockSpec((1, d), lambda i: (0, 0))], out_specs=x_spec,
        out_shape=jax.ShapeDtypeStruct((m, d), BF16),
        compiler_params=_params(("parallel",), VMEM_LIMIT), name="rmsnorm",
    )(x, g.reshape(1, d))


def rmsnorm_la(x, g, w_t, layer, la_row, rank, tr=512):
    m, d = x.shape
    x_spec = pl.BlockSpec((tr, d), lambda i: (i, 0))
    return pl.pallas_call(
        _norm_la_body, grid=(m // tr,),
        in_specs=[x_spec, pl.BlockSpec((1, d), lambda i: (0, 0)),
                  pl.BlockSpec((None, rank, d), lambda i: (layer, la_row // rank, 0))],
        out_specs=[x_spec, pl.BlockSpec((tr, LA_PAD), lambda i: (i, 0))],
        out_shape=[jax.ShapeDtypeStruct((m, d), BF16), jax.ShapeDtypeStruct((m, LA_PAD), F32)],
        scratch_shapes=[pltpu.VMEM((LA_PAD, d), BF16)],
        compiler_params=_params(("arbitrary",), VMEM_LIMIT), name="rmsnorm_la",
    )(x, g.reshape(1, d), w_t)


CAST_ROWS = 512


def _cast_tile(dst_ref, src_ref):
    def step(r, carry):
        rs = pl.ds(pl.multiple_of(r * CAST_ROWS, CAST_ROWS), CAST_ROWS)
        dst_ref[rs, :] = src_ref[rs, :].astype(dst_ref.dtype)
        return carry
    lax.fori_loop(0, src_ref.shape[0] // CAST_ROWS, step, 0)


def _in_proj_body(n_aligned, shift, a_ref, w_ref, wnext_ref, o_ref, wb_ref):
    j = pl.program_id(0)
    i = pl.program_id(1)
    tn = wb_ref.shape[0]
    step = 128

    @pl.when(jnp.logical_and(i == 0, j < n_aligned))
    def _():
        for r in range(0, tn, step):
            wb_ref[r:r + step, :] = w_ref[r:r + step, :].astype(wb_ref.dtype)

    @pl.when(jnp.logical_and(i == 0, j >= n_aligned))
    def _():
        for r in range(0, tn - step, step):
            wb_ref[r:r + step, :] = w_ref[r + shift:r + shift + step, :].astype(wb_ref.dtype)
        wb_ref[tn - step:tn - shift, :] = w_ref[tn - step + shift:tn, :].astype(wb_ref.dtype)
        wb_ref[tn - shift:tn, :] = wnext_ref[...].astype(wb_ref.dtype)

    o_ref[...] = lax.dot_general(a_ref[...], wb_ref[...], NT_DIMS, preferred_element_type=F32)


def in_proj(a, w_t, layer, skip_row, skip, tm=512, tn=1024):
    m, k = a.shape
    n = w_t.shape[1] - skip
    assert skip_row % tn == 0 and n % tn == 0 and tn % skip == 0 and skip % 16 == 0
    return pl.pallas_call(
        functools.partial(_in_proj_body, skip_row // tn, skip), grid=(n // tn, m // tm),
        in_specs=[pl.BlockSpec((tm, k), lambda j, i: (i, 0)),
                  pl.BlockSpec((None, tn, k), lambda j, i: (layer, j, 0)),
                  pl.BlockSpec((None, skip, k), lambda j, i: (layer, (j + 1) * (tn // skip), 0))],
        out_specs=pl.BlockSpec((tm, tn), lambda j, i: (i, j)),
        out_shape=jax.ShapeDtypeStruct((m, n), F32),
        scratch_shapes=[pltpu.VMEM((tn, k), BF16)],
        compiler_params=_params(("parallel", "arbitrary"), IN_PROJ_VMEM_LIMIT), name="in_proj",
    )(a, w_t, w_t)


def _mm_res_body(a_ref, b_ref, r_ref, o_ref):
    o_ref[...] = r_ref[...] + jnp.dot(a_ref[...], b_ref[...], preferred_element_type=F32)


def matmul_residual(a, b, layer, res, tm=1024, tn=512):
    m, k = a.shape
    n = b.shape[2]
    return pl.pallas_call(
        _mm_res_body, grid=(m // tm, n // tn),
        in_specs=[pl.BlockSpec((tm, k), lambda i, j: (i, 0)),
                  pl.BlockSpec((None, k, tn), lambda i, j: (layer, 0, j)),
                  pl.BlockSpec((tm, tn), lambda i, j: (i, j))],
        out_specs=pl.BlockSpec((tm, tn), lambda i, j: (i, j)),
        out_shape=jax.ShapeDtypeStruct((m, n), F32),
        compiler_params=_params(("parallel", "parallel"), VMEM_LIMIT), name="matmul_residual",
    )(a, b, res)


def _swiglu_body(a_ref, wg_ref, wu_ref, o_ref, wgb_ref, wub_ref):
    @pl.when(pl.program_id(1) == 0)
    def _():
        _cast_tile(wgb_ref, wg_ref)
        _cast_tile(wub_ref, wu_ref)

    a = a_ref[...]
    ug = jnp.dot(a, wgb_ref[...], preferred_element_type=F32)
    uu = jnp.dot(a, wub_ref[...], preferred_element_type=F32)
    o_ref[...] = (ug * jax.nn.sigmoid(ug) * uu).astype(o_ref.dtype)


def swiglu_up(a, w_gu, layer, tm=1024, tn=256):
    m, k = a.shape
    dff = w_gu.shape[2] // 2
    nb = dff // tn
    return pl.pallas_call(
        _swiglu_body, grid=(nb, m // tm),
        in_specs=[pl.BlockSpec((tm, k), lambda j, i: (i, 0)),
                  pl.BlockSpec((None, k, tn), lambda j, i: (layer, 0, j)),
                  pl.BlockSpec((None, k, tn), lambda j, i: (layer, 0, j + nb))],
        out_specs=pl.BlockSpec((tm, tn), lambda j, i: (i, j)),
        out_shape=jax.ShapeDtypeStruct((m, dff), BF16),
        scratch_shapes=[pltpu.VMEM((k, tn), BF16), pltpu.VMEM((k, tn), BF16)],
        compiler_params=_params(("parallel", "arbitrary"), VMEM_LIMIT), name="swiglu_up",
    )(a, w_gu, w_gu)


def _conv_body(n_prompt_groups, cb_ref, cc_ref, cx_ref, pc_ref, px_ref, cache_ref, w_ref,
               y_ref, tail_ref):
    g = pl.program_id(0)
    u = cc_ref[...] * cx_ref[...]
    rows = u.shape[0]
    prev = jnp.where(g > 0, pc_ref[...] * px_ref[...], 0.0)
    is_prompt = g < n_prompt_groups
    um2 = jnp.where(is_prompt, prev[6:7], cache_ref[0, 0:1])
    um1 = jnp.where(is_prompt, prev[7:8], cache_ref[0, 1:2])
    row = lax.broadcasted_iota(jnp.int32, u.shape, 0)
    u1 = jnp.where(row == 0, um1, pltpu.roll(u, 1, 0))
    u2 = jnp.where(row == 0, um2, jnp.where(row == 1, um1, pltpu.roll(u, 2, 0)))
    w = w_ref[...]
    y = u2 * w[0:1] + u1 * w[1:2] + u * w[2:3]
    y_ref[...] = (cb_ref[...] * y).astype(y_ref.dtype)
    tail_ref[0] = u[rows - 8:rows]


def gated_conv(z, cache, w, n_prompt_rows, group=CHUNK):
    m = z.shape[0]
    c = w.shape[1]
    ng = m // group
    npg = n_prompt_rows // group
    sub = group // 8
    row_spec = lambda col: pl.BlockSpec((group, c), lambda g: (g, col))
    prev_spec = lambda col: pl.BlockSpec((8, c), lambda g: (jnp.maximum(g * sub - 1, 0), col))
    return pl.pallas_call(
        functools.partial(_conv_body, npg), grid=(ng,),
        in_specs=[row_spec(0), row_spec(1), row_spec(2), prev_spec(1), prev_spec(2),
                  pl.BlockSpec((1, 2, c), lambda g: (jnp.maximum(g - npg, 0), 0, 0)),
                  pl.BlockSpec((3, c), lambda g: (0, 0))],
        out_specs=[pl.BlockSpec((group, c), lambda g: (g, 0)),
                   pl.BlockSpec((1, 8, c), lambda g: (g, 0, 0))],
        out_shape=[jax.ShapeDtypeStruct((m, c), BF16), jax.ShapeDtypeStruct((ng, 8, c), F32)],
        compiler_params=_params(("parallel",)), name="gated_conv",
    )(z, z, z, z, z, cache, w)


def _qk_body(heads, aq_ref, ak_ref, av_ref, gq_ref, gk_ref, qn_ref, kn_ref, knb_ref, vb_ref):
    dh = gq_ref.shape[1]
    gq = gq_ref[...]
    gk = gk_ref[...]
    for h in range(heads):
        sl = slice(h * dh, (h + 1) * dh)
        qn_ref[:, sl] = _rms(aq_ref[:, sl], gq).astype(qn_ref.dtype)
        kn = _rms(ak_ref[:, sl], gk)
        kn_ref[:, sl] = kn
        knb_ref[:, sl] = kn.astype(knb_ref.dtype)
    vb_ref[...] = av_ref[...].astype(vb_ref.dtype)


def qk_norm(z, g_q, g_k, col0, heads, tr=512):
    m = z.shape[0]
    dh = g_q.shape[0]
    c = heads * dh
    spec = lambda col: pl.BlockSpec((tr, c), lambda i: (i, col))
    g_spec = pl.BlockSpec((1, dh), lambda i: (0, 0))
    o_spec = pl.BlockSpec((tr, c), lambda i: (i, 0))
    return pl.pallas_call(
        functools.partial(_qk_body, heads), grid=(m // tr,),
        in_specs=[spec(col0), spec(col0 + 1), spec(col0 + 2), g_spec, g_spec],
        out_specs=[o_spec, o_spec, o_spec, o_spec],
        out_shape=[jax.ShapeDtypeStruct((m, c), BF16), jax.ShapeDtypeStruct((m, c), F32),
                   jax.ShapeDtypeStruct((m, c), BF16), jax.ShapeDtypeStruct((m, c), BF16)],
        compiler_params=_params(("parallel",)), name="qk_norm",
    )(z, z, z, g_q.reshape(1, dh), g_k.reshape(1, dh))


def _toeplitz_bias(vec, n_queries, n_keys):
    lanes = vec.shape[-1]
    return pltpu.roll(jnp.broadcast_to(vec, (n_queries, lanes)), 0, 1, stride=1, stride_axis=0)[:, :n_keys]


def _softmax_pv(s, v):
    p = jnp.exp(s - jnp.max(s, axis=-1, keepdims=True))
    p = p * (1.0 / jnp.sum(p, axis=-1, keepdims=True))
    return jnp.dot(p.astype(v.dtype), v, preferred_element_type=F32)


def _attn_body(n_prompt_tiles, chunks_per_tile, prompt_key_rows, scale,
               q_ref, k_ref, v_ref, bias_ref, tbias_ref, o_ref, tile_bias_ref):
    t = pl.program_id(1)
    tr = chunks_per_tile * CHUNK
    tk = tr + WINDOW

    @pl.when(t == 0)
    def _():
        row = lax.broadcasted_iota(jnp.int32, (tr, tk), 0)
        col = lax.broadcasted_iota(jnp.int32, (tr, tk), 1)
        first = jnp.left_shift(jnp.right_shift(row, CHUNK.bit_length() - 1), CHUNK.bit_length() - 1)
        in_band = jnp.logical_and(col >= first, col < first + SPAN)
        tile_bias_ref[...] = jnp.where(in_band, _toeplitz_bias(tbias_ref[0], tr, tk), NEG_INF)

    @pl.when(t < n_prompt_tiles)
    def _():
        start = pl.multiple_of(t * tr, tr)
        k = k_ref[pl.ds(start, tk), :]
        v = v_ref[pl.ds(start, tk), :]
        rg = tr // PROMPT_ROW_GROUPS
        col = lax.broadcasted_iota(jnp.int32, (rg, tk), 1)
        for r in range(0, tr, rg):
            s = lax.dot_general(q_ref[r:r + rg, :], k, NT_DIMS, preferred_element_type=F32)
            s = s * scale + tile_bias_ref[r:r + rg, :]
            s = jnp.where(col >= WINDOW - t * tr, s, NEG_INF)
            o_ref[r:r + rg, :] = _softmax_pv(s, v).astype(o_ref.dtype)

    @pl.when(t >= n_prompt_tiles)
    def _():
        bias = _toeplitz_bias(bias_ref[0], CHUNK, SPAN)
        for c in range(chunks_per_tile):
            n = (t - n_prompt_tiles) * chunks_per_tile + c
            start = pl.multiple_of(prompt_key_rows + n * SPAN, CHUNK)
            q = q_ref[c * CHUNK:(c + 1) * CHUNK, :]
            k = k_ref[pl.ds(start, SPAN), :]
            v = v_ref[pl.ds(start, SPAN), :]
            s = lax.dot_general(q, k, NT_DIMS, preferred_element_type=F32) * scale + bias
            o_ref[c * CHUNK:(c + 1) * CHUNK, :] = _softmax_pv(s, v).astype(o_ref.dtype)


def band_attention(qn, keys, vals, rel_bias, n_prompt_rows, chunks_per_tile=8):
    m, c = qn.shape
    heads = rel_bias.shape[0]
    dh = c // heads
    tr = CHUNK * chunks_per_tile
    tk = tr + WINDOW
    kr = keys.shape[0]
    assert n_prompt_rows % tr == 0 and (m - n_prompt_rows) % tr == 0
    body = functools.partial(_attn_body, n_prompt_rows // tr, chunks_per_tile,
                             n_prompt_rows + WINDOW, dh ** -0.5)
    return pl.pallas_call(
        body, grid=(heads, m // tr),
        in_specs=[pl.BlockSpec((tr, dh), lambda h, t: (t, h)),
                  pl.BlockSpec((kr, dh), lambda h, t: (0, h)),
                  pl.BlockSpec((kr, dh), lambda h, t: (0, h)),
                  pl.BlockSpec((1, 1, SPAN + CHUNK), lambda h, t: (h, 0, 0)),
                  pl.BlockSpec((1, 1, tk + tr), lambda h, t: (h, 0, 0))],
        out_specs=pl.BlockSpec((tr, dh), lambda h, t: (t, h)),
        out_shape=jax.ShapeDtypeStruct((m, c), BF16),
        scratch_shapes=[pltpu.VMEM((tr, tk), F32)],
        compiler_params=_params(("parallel", "arbitrary"), VMEM_LIMIT), name="band_attention",
    )(qn, keys, vals, rel_bias_rows(rel_bias, CHUNK, SPAN), rel_bias_rows(rel_bias, tr, tk))


def rel_bias_rows(rel_bias, n_queries, n_keys):
    lanes = n_keys + n_queries
    i = jnp.arange(lanes)
    dist = jnp.where(i < n_keys, -i, lanes - i)
    rel = jnp.clip(dist + WINDOW, -REL_CLIP, REL_CLIP) + REL_CLIP
    return rel_bias.astype(F32)[:, rel].reshape(rel_bias.shape[0], 1, lanes)


def _gla_body(n_prompt_tiles, seg_rows, lq_ref, lk_ref, lv_ref, lr_ref, la_ref, wa2_ref, ba_ref, g_ref,
              s0_ref, y_ref, soutp_ref, souts_ref, state_ref, o_ref):
    t = pl.program_id(0)
    rows = lq_ref.shape[0]
    heads, dv, dk = state_ref.shape
    cb = GLA_CHUNK
    nc = rows // cb
    chunks_per_seg = seg_rows // cb
    nseg = rows // seg_rows
    is_sample = t >= n_prompt_tiles
    shift = cb.bit_length() - 1

    r_i = lax.broadcasted_iota(jnp.int32, (rows, rows), 0)
    c_i = lax.broadcasted_iota(jnp.int32, (rows, rows), 1)
    same_chunk = jnp.right_shift(r_i, shift) == jnp.right_shift(c_i, shift)
    chunk_tri = jnp.logical_and(same_chunk, r_i >= c_i).astype(F32)
    causal = (lax.broadcasted_iota(jnp.int32, (cb, cb), 0)
              >= lax.broadcasted_iota(jnp.int32, (cb, cb), 1))[None]

    kd = heads * dk
    x = jnp.dot(la_ref[...], wa2_ref[...], precision=HIGHEST, preferred_element_type=F32) + ba_ref[...]
    log_a = (jnp.minimum(x, 0.0) - jnp.log1p(jnp.exp(-jnp.abs(x)))) * (1.0 / GLA_TAU)
    b = jnp.dot(chunk_tri, log_a, precision=HIGHEST, preferred_element_type=F32).reshape(nc, cb, kd)
    b_mid = b[:, cb // 2 - 1:cb // 2]
    b_last = b[:, cb - 1:cb]
    q = (lq_ref[...] * (dk ** -0.5)).reshape(nc, cb, kd)
    k = lk_ref[...].reshape(nc, cb, kd)
    q_dec_all = (q * jnp.exp(b)).astype(BF16)
    q_mid_all = (q * jnp.exp(b - b_mid)).astype(BF16)
    k_mid_all = (k * jnp.exp(b_mid - b)).astype(BF16)
    k_rem_all = (k * jnp.exp(b_last - b)).astype(BF16)
    decay_all = jnp.exp(b_last)
    pre = []
    for h in range(heads):
        ks = slice(h * dk, (h + 1) * dk)
        v = lv_ref[:, h * dv:(h + 1) * dv].astype(BF16).reshape(nc, cb, dv)
        att = jnp.einsum("clk,cmk->clm", q_mid_all[:, :, ks], k_mid_all[:, :, ks],
                         preferred_element_type=F32)
        att = jnp.where(causal, att, 0.0).astype(BF16)
        o_intra = jnp.einsum("clm,cmv->clv", att, v, preferred_element_type=F32)
        pre.append((q_dec_all[:, :, ks], k_rem_all[:, :, ks], v, o_intra, decay_all[:, :, ks]))

    for s in range(nseg):
        @pl.when(is_sample)
        def _():
            for h in range(heads):
                state_ref[h] = s0_ref[s, h].T

        if s == 0:
            @pl.when(t == 0)
            def _():
                state_ref[...] = jnp.zeros_like(state_ref)

        for c in range(s * chunks_per_seg, (s + 1) * chunks_per_seg):
            for h in range(heads):
                q_dec, k_rem, v, o_intra, decay = pre[h]
                state = state_ref[h]
                o = o_intra[c] + lax.dot_general(q_dec[c], state.astype(BF16), NT_DIMS,
                                                 preferred_element_type=F32)
                kv = lax.dot_general(v[c], k_rem[c], TN_DIMS, preferred_element_type=F32)
                state_ref[h] = decay[c] * state + kv
                o_ref[c * cb:(c + 1) * cb, h * dv:(h + 1) * dv] = o

        @pl.when(is_sample)
        def _():
            for h in range(heads):
                souts_ref[s, h] = state_ref[h].T

        if s == nseg - 1:
            @pl.when(t == n_prompt_tiles - 1)
            def _():
                for h in range(heads):
                    soutp_ref[0, h] = state_ref[h].T

    g = g_ref[...]
    for h in range(heads):
        vs = slice(h * dv, (h + 1) * dv)
        r = lr_ref[:, vs]
        y_ref[:, vs] = (_rms(o_ref[:, vs], g) * (r * jax.nn.sigmoid(r))).astype(y_ref.dtype)


def gla(z, la, w_a2, b_a, g_gla, s0, n_prompt_rows, seg_rows, col_q, col_k, col_v, col_r, tr=128):
    m = z.shape[0]
    nseq, heads, dk, dv = s0.shape
    kd, vd = heads * dk, heads * dv
    npt = n_prompt_rows // tr
    spt = tr // seg_rows
    assert n_prompt_rows % tr == 0 and tr % seg_rows == 0 and seg_rows % GLA_CHUNK == 0
    sample_tile = lambda t: (jnp.maximum(t - npt, 0), 0, 0, 0)
    wa2p = jnp.zeros((LA_PAD, kd), F32).at[:w_a2.shape[0]].set(w_a2)
    return pl.pallas_call(
        functools.partial(_gla_body, npt, seg_rows), grid=(m // tr,),
        in_specs=[pl.BlockSpec((tr, kd), lambda t: (t, col_q // kd)),
                  pl.BlockSpec((tr, kd), lambda t: (t, col_k // kd)),
                  pl.BlockSpec((tr, vd), lambda t: (t, col_v // vd)),
                  pl.BlockSpec((tr, vd), lambda t: (t, col_r // vd)),
                  pl.BlockSpec((tr, LA_PAD), lambda t: (t, 0)),
                  pl.BlockSpec((LA_PAD, kd), lambda t: (0, 0)),
                  pl.BlockSpec((1, kd), lambda t: (0, 0)),
                  pl.BlockSpec((1, dv), lambda t: (0, 0)),
                  pl.BlockSpec((spt, heads, dk, dv), sample_tile)],
        out_specs=[pl.BlockSpec((tr, vd), lambda t: (t, 0)),
                   pl.BlockSpec((1, heads, dk, dv), lambda t: (0, 0, 0, 0)),
                   pl.BlockSpec((spt, heads, dk, dv), sample_tile)],
        out_shape=[jax.ShapeDtypeStruct((m, vd), BF16),
                   jax.ShapeDtypeStruct((1, heads, dk, dv), F32),
                   jax.ShapeDtypeStruct((nseq, heads, dk, dv), F32)],
        scratch_shapes=[pltpu.VMEM((heads, dv, dk), F32), pltpu.VMEM((tr, vd), F32)],
        compiler_params=_params(("arbitrary",), VMEM_LIMIT), name="gla",
    )(z, z, z, z, la, wa2p, b_a.reshape(1, -1), g_gla.reshape(1, -1), s0)


def _branch_body(ya_ref, yb_ref, yc_ref, wa_ref, wb_ref, wc_ref, ga_ref, gb_ref, gc_ref, o_ref):
    m = jax.nn.sigmoid(ga_ref[...]) * jnp.dot(ya_ref[...], wa_ref[...], preferred_element_type=F32)
    m = m + jax.nn.sigmoid(gb_ref[...]) * jnp.dot(yb_ref[...], wb_ref[...], preferred_element_type=F32)
    m = m + jax.nn.sigmoid(gc_ref[...]) * jnp.dot(yc_ref[...], wc_ref[...], preferred_element_type=F32)
    o_ref[...] = m.astype(o_ref.dtype)


def branch_merge(ya, yb, yc, w_branch, layer, z, gate_col, tm=1024, tn=512):
    m = ya.shape[0]
    ca, cb, cc = ya.shape[1], yb.shape[1], yc.shape[1]
    d = w_branch.shape[2]
    assert ca == cb and cc % ca == 0 and (ca + cb) % cc == 0
    gb = gate_col // tn
    nd = d // tn
    return pl.pallas_call(
        _branch_body, grid=(m // tm, nd),
        in_specs=[pl.BlockSpec((tm, ca), lambda i, j: (i, 0)),
                  pl.BlockSpec((tm, cb), lambda i, j: (i, 0)),
                  pl.BlockSpec((tm, cc), lambda i, j: (i, 0)),
                  pl.BlockSpec((None, ca, tn), lambda i, j: (layer, 0, j)),
                  pl.BlockSpec((None, cb, tn), lambda i, j: (layer, 1, j)),
                  pl.BlockSpec((None, cc, tn), lambda i, j: (layer, (ca + cb) // cc, j)),
                  pl.BlockSpec((tm, tn), lambda i, j: (i, gb + j)),
                  pl.BlockSpec((tm, tn), lambda i, j: (i, gb + nd + j)),
                  pl.BlockSpec((tm, tn), lambda i, j: (i, gb + 2 * nd + j))],
        out_specs=pl.BlockSpec((tm, tn), lambda i, j: (i, j)),
        out_shape=jax.ShapeDtypeStruct((m, d), BF16),
        compiler_params=_params(("parallel", "parallel"), VMEM_LIMIT), name="branch_merge",
    )(ya, yb, yc, w_branch, w_branch, w_branch, z, z, z)


def kernel(x_prompt, x_sample, cache_conv, cache_k, cache_v, state_gla, g_mix, w_in, conv_w, g_q, g_k,
           rel_bias, w_a2, b_a, g_gla, w_branch, w_out, g_ffn, w_gu, w_down):
    nb, seq, d = x_prompt.shape
    nd, ds, _ = x_sample.shape
    depth = g_mix.shape[0]
    assert nb == 1
    mp = nb * seq
    ms = nd * ds
    conv_dim = conv_w.shape[2]
    att_dim = ATT_HEADS * ATT_HEAD_DIM
    kdim = GLA_HEADS * GLA_DK
    vdim = GLA_HEADS * GLA_DV
    col_att = 3 * conv_dim
    col_lq = col_att + 3 * att_dim
    col_lk = col_lq + kdim
    col_lv = col_lk + kdim
    col_lr = col_lv + vdim
    col_la = col_lr + vdim
    col_gate = col_la
    keep = min(WINDOW, seq)

    x = jnp.concatenate([x_prompt.reshape(mp, d), x_sample.reshape(ms, d)], axis=0)
    w_branch_b = w_branch.astype(BF16)
    w_out_b = w_out.astype(BF16)
    w_down_b = w_down.astype(BF16)
    outs = {name: [] for name in ("conv_p", "k_p", "v_p", "gla_p", "conv_s", "k_s", "v_s", "gla_s")}
    w_in_t = jnp.swapaxes(w_in, 1, 2)
    for l in range(depth):
        h, la = rmsnorm_la(x, g_mix[l], w_in_t, l, col_la, GLA_RANK)
        z = in_proj(h, w_in_t, l, col_la, GLA_RANK)

        y_a, conv_tail = gated_conv(z, cache_conv[l], conv_w[l], mp)

        qn, kn, knb, vb = qk_norm(z, g_q[l], g_k[l], col_att // att_dim, ATT_HEADS)
        zpad = jnp.zeros((WINDOW, att_dim), BF16)
        keys = jnp.concatenate(
            [zpad, knb[:mp],
             jnp.concatenate([cache_k[l].reshape(nd, WINDOW, att_dim).astype(BF16),
                              knb[mp:].reshape(nd, ds, att_dim)], axis=1).reshape(nd * SPAN, att_dim)], axis=0)
        vals = jnp.concatenate(
            [zpad, vb[:mp],
             jnp.concatenate([cache_v[l].reshape(nd, WINDOW, att_dim).astype(BF16),
                              vb[mp:].reshape(nd, ds, att_dim)], axis=1).reshape(nd * SPAN, att_dim)], axis=0)
        y_b = band_attention(qn, keys, vals, rel_bias[l], mp)

        y_c, s_prompt, s_sample = gla(z, la, w_a2[l], b_a[l], g_gla[l], state_gla[l], mp, ds,
                                      col_lq, col_lk, col_lv, col_lr)

        mrg = branch_merge(y_a, y_b, y_c, w_branch_b, l, z, col_gate)
        x = matmul_residual(mrg, w_out_b, l, x)

        hf = rmsnorm_bf16(x, g_ffn[l])
        hid = swiglu_up(hf, w_gu, l)
        x = matmul_residual(hid, w_down_b, l, x, tm=512, tn=512)

        v_new = z[:, col_att + 2 * att_dim:col_att + 3 * att_dim]
        outs["conv_p"].append(conv_tail[mp // CHUNK - 1, 6:8][None])
        outs["conv_s"].append(conv_tail[mp // CHUNK:, 6:8])
        outs["k_p"].append(kn[mp - keep:mp].reshape(nb, keep, ATT_HEADS, ATT_HEAD_DIM))
        outs["v_p"].append(v_new[mp - keep:mp].reshape(nb, keep, ATT_HEADS, ATT_HEAD_DIM))
        outs["k_s"].append(kn[mp:].reshape(nd, ds, ATT_HEADS, ATT_HEAD_DIM))
        outs["v_s"].append(v_new[mp:].reshape(nd, ds, ATT_HEADS, ATT_HEAD_DIM))
        outs["gla_p"].append(s_prompt)
        outs["gla_s"].append(s_sample)

    st = {name: jnp.stack(v) for name, v in outs.items()}
    return (x[:mp].reshape(nb, seq, d), x[mp:].reshape(nd, ds, d),
            st["conv_p"], st["k_p"], st["v_p"], st["gla_p"],
            st["conv_s"], st["k_s"], st["v_s"], st["gla_s"])
```

```python
import functools

import jax
import jax.numpy as jnp
from jax import lax
from jax.experimental import pallas as pl
from jax.experimental.pallas import tpu as pltpu

F32 = jnp.float32
BF16 = jnp.bfloat16
HIGHEST = lax.Precision.HIGHEST

CHUNK = 64
BAND_CHUNKS = 8
WINDOW = BAND_CHUNKS * CHUNK
SPAN = WINDOW + CHUNK
REL_CLIP = 128
ATT_HEADS = 8
ATT_HEAD_DIM = 128
GLA_HEADS = 4
GLA_DK = 256
GLA_DV = 512
GLA_RANK = 16
GLA_TAU = 16.0
GLA_CHUNK = 32
LA_PAD = 128
PROMPT_ROW_GROUPS = 2
NEG_INF = -1e30
EPS = 1e-6
VMEM_LIMIT = 56 * 1024 * 1024
IN_PROJ_VMEM_LIMIT = 60 * 1024 * 1024

NT_DIMS = (((1,), (1,)), ((), ()))
TN_DIMS = (((0,), (0,)), ((), ()))


def _params(sem, vmem=None):
    return pltpu.CompilerParams(dimension_semantics=sem, vmem_limit_bytes=vmem)


def _rms(x, g):
    return x * lax.rsqrt(jnp.mean(x * x, axis=-1, keepdims=True) + EPS) * g


def _split_row_specs(tr, tn, n_prompt_tiles, row_axis, col_of):
    def prompt_map(*ids):
        return (jnp.minimum(ids[row_axis], n_prompt_tiles - 1), col_of(*ids))

    def sample_map(*ids):
        return (jnp.maximum(ids[row_axis] - n_prompt_tiles, 0), col_of(*ids))

    return pl.BlockSpec((tr, tn), prompt_map), pl.BlockSpec((tr, tn), sample_map)


def _pick_rows(i, n_prompt_tiles, p_ref, s_ref):
    return jnp.where(i < n_prompt_tiles, p_ref[...], s_ref[...])


def _norm_body(n_prompt_tiles, xp_ref, xs_ref, g_ref, h_ref):
    x = _pick_rows(pl.program_id(0), n_prompt_tiles, xp_ref, xs_ref)
    h_ref[...] = _rms(x, g_ref[...]).astype(h_ref.dtype)


def _norm_la_body(n_prompt_tiles, xp_ref, xs_ref, g_ref, wla_ref, h_ref, la_ref, wlab_ref):
    rank = wla_ref.shape[0]

    @pl.when(pl.program_id(0) == 0)
    def _():
        wlab_ref[...] = jnp.zeros_like(wlab_ref)
        wlab_ref[0:rank, :] = wla_ref[...].astype(wlab_ref.dtype)

    x = _pick_rows(pl.program_id(0), n_prompt_tiles, xp_ref, xs_ref)
    h = _rms(x, g_ref[...]).astype(h_ref.dtype)
    h_ref[...] = h
    la_ref[...] = lax.dot_general(h, wlab_ref[...], NT_DIMS, preferred_element_type=F32)


def rmsnorm_bf16(xp, xs, g, tr=512):
    (mp, d), ms = xp.shape, xs.shape[0]
    assert mp % tr == 0 and ms % tr == 0
    p_spec, s_spec = _split_row_specs(tr, d, mp // tr, 0, lambda i: 0)
    return pl.pallas_call(
        functools.partial(_norm_body, mp // tr), grid=((mp + ms) // tr,),
        in_specs=[p_spec, s_spec, pl.BlockSpec((1, d), lambda i: (0, 0))],
        out_specs=pl.BlockSpec((tr, d), lambda i: (i, 0)),
        out_shape=jax.ShapeDtypeStruct((mp + ms, d), BF16),
        compiler_params=_params(("parallel",), VMEM_LIMIT), name="rmsnorm",
    )(xp, xs, g.reshape(1, d))


def rmsnorm_la(xp, xs, g, w_t, layer, la_row, rank, tr=512):
    (mp, d), ms = xp.shape, xs.shape[0]
    assert mp % tr == 0 and ms % tr == 0
    m = mp + ms
    p_spec, s_spec = _split_row_specs(tr, d, mp // tr, 0, lambda i: 0)
    return pl.pallas_call(
        functools.partial(_norm_la_body, mp // tr), grid=(m // tr,),
        in_specs=[p_spec, s_spec, pl.BlockSpec((1, d), lambda i: (0, 0)),
                  pl.BlockSpec((None, rank, d), lambda i: (layer, la_row // rank, 0))],
        out_specs=[pl.BlockSpec((tr, d), lambda i: (i, 0)), pl.BlockSpec((tr, LA_PAD), lambda i: (i, 0))],
        out_shape=[jax.ShapeDtypeStruct((m, d), BF16), jax.ShapeDtypeStruct((m, LA_PAD), F32)],
        scratch_shapes=[pltpu.VMEM((LA_PAD, d), BF16)],
        compiler_params=_params(("arbitrary",), VMEM_LIMIT), name="rmsnorm_la",
    )(xp, xs, g.reshape(1, d), w_t)


CAST_ROWS = 512


def _cast_tile(dst_ref, src_ref):
    def step(r, carry):
        rs = pl.ds(pl.multiple_of(r * CAST_ROWS, CAST_ROWS), CAST_ROWS)
        dst_ref[rs, :] = src_ref[rs, :].astype(dst_ref.dtype)
        return carry
    lax.fori_loop(0, src_ref.shape[0] // CAST_ROWS, step, 0)


def _in_proj_body(n_aligned, shift, a_ref, w_ref, wnext_ref, o_ref, wb_ref):
    j = pl.program_id(0)
    i = pl.program_id(1)
    tn = wb_ref.shape[0]
    step = 128

    @pl.when(jnp.logical_and(i == 0, j < n_aligned))
    def _():
        for r in range(0, tn, step):
            wb_ref[r:r + step, :] = w_ref[r:r + step, :].astype(wb_ref.dtype)

    @pl.when(jnp.logical_and(i == 0, j >= n_aligned))
    def _():
        for r in range(0, tn - step, step):
            wb_ref[r:r + step, :] = w_ref[r + shift:r + shift + step, :].astype(wb_ref.dtype)
        wb_ref[tn - step:tn - shift, :] = w_ref[tn - step + shift:tn, :].astype(wb_ref.dtype)
        wb_ref[tn - shift:tn, :] = wnext_ref[...].astype(wb_ref.dtype)

    o_ref[...] = lax.dot_general(a_ref[...], wb_ref[...], NT_DIMS, preferred_element_type=F32)


def in_proj(a, w_t, layer, skip_row, skip, tm=512, tn=1024):
    m, k = a.shape
    n = w_t.shape[1] - skip
    assert skip_row % tn == 0 and n % tn == 0 and tn % skip == 0 and skip % 16 == 0
    return pl.pallas_call(
        functools.partial(_in_proj_body, skip_row // tn, skip), grid=(n // tn, m // tm),
        in_specs=[pl.BlockSpec((tm, k), lambda j, i: (i, 0)),
                  pl.BlockSpec((None, tn, k), lambda j, i: (layer, j, 0)),
                  pl.BlockSpec((None, skip, k), lambda j, i: (layer, (j + 1) * (tn // skip), 0))],
        out_specs=pl.BlockSpec((tm, tn), lambda j, i: (i, j)),
        out_shape=jax.ShapeDtypeStruct((m, n), F32),
        scratch_shapes=[pltpu.VMEM((tn, k), BF16)],
        compiler_params=_params(("parallel", "arbitrary"), IN_PROJ_VMEM_LIMIT), name="in_proj",
    )(a, w_t, w_t)


def _mm_res_body(n_prompt_tiles, a_ref, b_ref, rp_ref, rs_ref, op_ref, os_ref):
    i = pl.program_id(1)
    acc = jnp.dot(a_ref[...], b_ref[...], preferred_element_type=F32)

    @pl.when(i < n_prompt_tiles)
    def _():
        op_ref[...] = rp_ref[...] + acc

    @pl.when(i >= n_prompt_tiles)
    def _():
        os_ref[...] = rs_ref[...] + acc


def matmul_residual(a, b, layer, res_p, res_s, tm=1024, tn=512):
    m, k = a.shape
    n = b.shape[2]
    mp, ms = res_p.shape[0], res_s.shape[0]
    assert mp % tm == 0 and ms % tm == 0 and mp + ms == m
    p_spec, s_spec = _split_row_specs(tm, tn, mp // tm, 1, lambda j, i: j)
    return pl.pallas_call(
        functools.partial(_mm_res_body, mp // tm), grid=(n // tn, m // tm),
        in_specs=[pl.BlockSpec((tm, k), lambda j, i: (i, 0)),
                  pl.BlockSpec((None, k, tn), lambda j, i: (layer, 0, j)),
                  p_spec, s_spec],
        out_specs=[p_spec, s_spec],
        out_shape=[jax.ShapeDtypeStruct((mp, n), F32), jax.ShapeDtypeStruct((ms, n), F32)],
        compiler_params=_params(("parallel", "arbitrary"), VMEM_LIMIT), name="matmul_residual",
    )(a, b, res_p, res_s)


def _swiglu_body(a_ref, wg_ref, wu_ref, o_ref, wgb_ref, wub_ref):
    @pl.when(pl.program_id(1) == 0)
    def _():
        _cast_tile(wgb_ref, wg_ref)
        _cast_tile(wub_ref, wu_ref)

    a = a_ref[...]
    ug = jnp.dot(a, wgb_ref[...], preferred_element_type=F32)
    uu = jnp.dot(a, wub_ref[...], preferred_element_type=F32)
    o_ref[...] = (ug * jax.nn.sigmoid(ug) * uu).astype(o_ref.dtype)


def swiglu_up(a, w_gu, layer, tm=1536, tn=256):
    m, k = a.shape
    dff = w_gu.shape[2] // 2
    nb = dff // tn
    return pl.pallas_call(
        _swiglu_body, grid=(nb, m // tm),
        in_specs=[pl.BlockSpec((tm, k), lambda j, i: (i, 0)),
                  pl.BlockSpec((None, k, tn), lambda j, i: (layer, 0, j)),
                  pl.BlockSpec((None, k, tn), lambda j, i: (layer, 0, j + nb))],
        out_specs=pl.BlockSpec((tm, tn), lambda j, i: (i, j)),
        out_shape=jax.ShapeDtypeStruct((m, dff), BF16),
        scratch_shapes=[pltpu.VMEM((k, tn), BF16), pltpu.VMEM((k, tn), BF16)],
        compiler_params=_params(("parallel", "arbitrary"), VMEM_LIMIT), name="swiglu_up",
    )(a, w_gu, w_gu)


def _conv_body(n_prompt_tiles, group, cb_ref, cc_ref, cx_ref, pc_ref, px_ref, cache_ref, w_ref,
               y_ref, tail_ref):
    t = pl.program_id(0)
    u = cc_ref[...] * cx_ref[...]
    rows = u.shape[0]
    row = lax.broadcasted_iota(jnp.int32, u.shape, 0)
    u1 = pltpu.roll(u, 1, 0)
    u2 = pltpu.roll(u, 2, 0)
    w = w_ref[...]

    def finish(um1, um2):
        y = um2 * w[0:1] + um1 * w[1:2] + u * w[2:3]
        y_ref[...] = (cb_ref[...] * y).astype(y_ref.dtype)

    @pl.when(t < n_prompt_tiles)
    def _():
        prev = jnp.where(t > 0, pc_ref[...] * px_ref[...], 0.0)
        finish(jnp.where(row == 0, prev[7:8], u1),
               jnp.where(row == 0, prev[6:7], jnp.where(row == 1, prev[7:8], u2)))

    @pl.when(t >= n_prompt_tiles)
    def _():
        um1, um2 = u1, u2
        for g in range(rows // group):
            c0 = cache_ref[g, 0:1]
            c1 = cache_ref[g, 1:2]
            um1 = jnp.where(row == g * group, c1, um1)
            um2 = jnp.where(row == g * group, c0, jnp.where(row == g * group + 1, c1, um2))
        finish(um1, um2)

    for g in range(rows // group):
        tail_ref[g] = u[(g + 1) * group - 8:(g + 1) * group]


def gated_conv(z, cache, w, n_prompt_rows, group=CHUNK, tr=512):
    m = z.shape[0]
    c = w.shape[1]
    gpt = tr // group
    npt = n_prompt_rows // tr
    assert n_prompt_rows % tr == 0 and m % tr == 0 and tr % group == 0
    row_spec = lambda col: pl.BlockSpec((tr, c), lambda t: (t, col))
    prev_spec = lambda col: pl.BlockSpec((8, c), lambda t: (jnp.maximum(t * (tr // 8) - 1, 0), col))
    return pl.pallas_call(
        functools.partial(_conv_body, npt, group), grid=(m // tr,),
        in_specs=[row_spec(0), row_spec(1), row_spec(2), prev_spec(1), prev_spec(2),
                  pl.BlockSpec((gpt, 2, c), lambda t: (jnp.maximum(t - npt, 0), 0, 0)),
                  pl.BlockSpec((3, c), lambda t: (0, 0))],
        out_specs=[pl.BlockSpec((tr, c), lambda t: (t, 0)),
                   pl.BlockSpec((gpt, 8, c), lambda t: (t, 0, 0))],
        out_shape=[jax.ShapeDtypeStruct((m, c), BF16), jax.ShapeDtypeStruct((m // group, 8, c), F32)],
        compiler_params=_params(("parallel",), VMEM_LIMIT), name="gated_conv",
    )(z, z, z, z, z, cache, w)


def _qk_body(heads, aq_ref, ak_ref, av_ref, gq_ref, gk_ref, qn_ref, kn_ref, kpad_ref, vpad_ref):
    s = pl.program_id(0)

    @pl.when(s == 0)
    def _():
        kpad_ref[...] = jnp.zeros_like(kpad_ref)
        vpad_ref[...] = jnp.zeros_like(vpad_ref)

    @pl.when(s > 0)
    def _():
        dh = gq_ref.shape[1]
        gq = gq_ref[...]
        gk = gk_ref[...]
        for h in range(heads):
            sl = slice(h * dh, (h + 1) * dh)
            qn_ref[:, sl] = _rms(aq_ref[:, sl], gq).astype(qn_ref.dtype)
            kn = _rms(ak_ref[:, sl], gk)
            kn_ref[:, sl] = kn
            kpad_ref[:, sl] = kn.astype(kpad_ref.dtype)
        vpad_ref[...] = av_ref[...].astype(vpad_ref.dtype)


def qk_norm(z, g_q, g_k, col0, heads, tr=WINDOW):
    m = z.shape[0]
    dh = g_q.shape[0]
    c = heads * dh
    assert tr == WINDOW and m % tr == 0
    tile = lambda s: jnp.maximum(s - 1, 0)
    spec = lambda col: pl.BlockSpec((tr, c), lambda s: (tile(s), col))
    g_spec = pl.BlockSpec((1, dh), lambda s: (0, 0))
    o_spec = pl.BlockSpec((tr, c), lambda s: (tile(s), 0))
    pad_spec = pl.BlockSpec((tr, c), lambda s: (s, 0))
    return pl.pallas_call(
        functools.partial(_qk_body, heads), grid=(1 + m // tr,),
        in_specs=[spec(col0), spec(col0 + 1), spec(col0 + 2), g_spec, g_spec],
        out_specs=[o_spec, o_spec, pad_spec, pad_spec],
        out_shape=[jax.ShapeDtypeStruct((m, c), BF16), jax.ShapeDtypeStruct((m, c), F32),
                   jax.ShapeDtypeStruct((WINDOW + m, c), BF16), jax.ShapeDtypeStruct((WINDOW + m, c), BF16)],
        compiler_params=_params(("arbitrary",), VMEM_LIMIT), name="qk_norm",
    )(z, z, z, g_q.reshape(1, dh), g_k.reshape(1, dh))


def _toeplitz_bias(vec, n_queries, n_keys):
    lanes = vec.shape[-1]
    return pltpu.roll(jnp.broadcast_to(vec, (n_queries, lanes)), 0, 1, stride=1, stride_axis=0)[:, :n_keys]


def _softmax_pv(s, v):
    p = jnp.exp(s - jnp.max(s, axis=-1, keepdims=True))
    p = p * (1.0 / jnp.sum(p, axis=-1, keepdims=True))
    return jnp.dot(p.astype(v.dtype), v, preferred_element_type=F32)


def _attn_body(n_prompt_tiles, chunks_per_tile, prompt_key_rows, scale,
               q_ref, k_ref, v_ref, ck_ref, cv_ref, bias_ref, tbias_ref, o_ref, tile_bias_ref):
    t = pl.program_id(1)
    tr = chunks_per_tile * CHUNK
    tk = tr + WINDOW

    @pl.when(t == 0)
    def _():
        row = lax.broadcasted_iota(jnp.int32, (tr, tk), 0)
        col = lax.broadcasted_iota(jnp.int32, (tr, tk), 1)
        first = jnp.left_shift(jnp.right_shift(row, CHUNK.bit_length() - 1), CHUNK.bit_length() - 1)
        in_band = jnp.logical_and(col >= first, col < first + SPAN)
        tile_bias_ref[...] = jnp.where(in_band, _toeplitz_bias(tbias_ref[0], tr, tk), NEG_INF)

    @pl.when(t < n_prompt_tiles)
    def _():
        start = pl.multiple_of(t * tr, tr)
        k = k_ref[pl.ds(start, tk), :]
        v = v_ref[pl.ds(start, tk), :]
        rg = tr // PROMPT_ROW_GROUPS
        col = lax.broadcasted_iota(jnp.int32, (rg, tk), 1)
        for r in range(0, tr, rg):
            s = lax.dot_general(q_ref[r:r + rg, :], k, NT_DIMS, preferred_element_type=F32)
            s = s * scale + tile_bias_ref[r:r + rg, :]
            s = jnp.where(col >= WINDOW - t * tr, s, NEG_INF)
            o_ref[r:r + rg, :] = _softmax_pv(s, v).astype(o_ref.dtype)

    @pl.when(t >= n_prompt_tiles)
    def _():
        bias = _toeplitz_bias(bias_ref[0], CHUNK, SPAN)
        for c in range(chunks_per_tile):
            n = (t - n_prompt_tiles) * chunks_per_tile + c
            old = pl.ds(pl.multiple_of(n * WINDOW, WINDOW), WINDOW)
            new = pl.ds(pl.multiple_of(prompt_key_rows + n * CHUNK, CHUNK), CHUNK)
            q = q_ref[c * CHUNK:(c + 1) * CHUNK, :]
            k = jnp.concatenate([ck_ref[old, :].astype(k_ref.dtype), k_ref[new, :]], axis=0)
            v = jnp.concatenate([cv_ref[old, :].astype(v_ref.dtype), v_ref[new, :]], axis=0)
            s = lax.dot_general(q, k, NT_DIMS, preferred_element_type=F32) * scale + bias
            o_ref[c * CHUNK:(c + 1) * CHUNK, :] = _softmax_pv(s, v).astype(o_ref.dtype)


def band_attention(qn, keys, vals, cache_k, cache_v, layer, rel_bias, n_prompt_rows, chunks_per_tile=8):
    m, c = qn.shape
    heads = rel_bias.shape[0]
    dh = c // heads
    tr = CHUNK * chunks_per_tile
    tk = tr + WINDOW
    kr = keys.shape[0]
    cr = cache_k.shape[1]
    assert n_prompt_rows % tr == 0 and (m - n_prompt_rows) % tr == 0
    assert kr == WINDOW + m and cr == (m - n_prompt_rows) // CHUNK * WINDOW
    body = functools.partial(_attn_body, n_prompt_rows // tr, chunks_per_tile,
                             n_prompt_rows + WINDOW, dh ** -0.5)
    return pl.pallas_call(
        body, grid=(heads, m // tr),
        in_specs=[pl.BlockSpec((tr, dh), lambda h, t: (t, h)),
                  pl.BlockSpec((kr, dh), lambda h, t: (0, h)),
                  pl.BlockSpec((kr, dh), lambda h, t: (0, h)),
                  pl.BlockSpec((None, cr, dh), lambda h, t: (layer, 0, h)),
                  pl.BlockSpec((None, cr, dh), lambda h, t: (layer, 0, h)),
                  pl.BlockSpec((1, 1, SPAN + CHUNK), lambda h, t: (h, 0, 0)),
                  pl.BlockSpec((1, 1, tk + tr), lambda h, t: (h, 0, 0))],
        out_specs=pl.BlockSpec((tr, dh), lambda h, t: (t, h)),
        out_shape=jax.ShapeDtypeStruct((m, c), BF16),
        scratch_shapes=[pltpu.VMEM((tr, tk), F32)],
        compiler_params=_params(("parallel", "arbitrary"), VMEM_LIMIT), name="band_attention",
    )(qn, keys, vals, cache_k, cache_v,
      rel_bias_rows(rel_bias, CHUNK, SPAN), rel_bias_rows(rel_bias, tr, tk))


def rel_bias_rows(rel_bias, n_queries, n_keys):
    lanes = n_keys + n_queries
    i = jnp.arange(lanes)
    dist = jnp.where(i < n_keys, -i, lanes - i)
    rel = jnp.clip(dist + WINDOW, -REL_CLIP, REL_CLIP) + REL_CLIP
    return rel_bias.astype(F32)[:, rel].reshape(rel_bias.shape[0], 1, lanes)


def _gla_body(n_prompt_tiles, seg_rows, lq_ref, lk_ref, lv_ref, lr_ref, la_ref, wa2_ref, ba_ref, g_ref,
              s0_ref, y_ref, soutp_ref, souts_ref, state_ref, o_ref):
    t = pl.program_id(0)
    rows = lq_ref.shape[0]
    heads, dv, dk = state_ref.shape
    cb = GLA_CHUNK
    nc = rows // cb
    chunks_per_seg = seg_rows // cb
    nseg = rows // seg_rows
    is_sample = t >= n_prompt_tiles
    shift = cb.bit_length() - 1

    r_i = lax.broadcasted_iota(jnp.int32, (rows, rows), 0)
    c_i = lax.broadcasted_iota(jnp.int32, (rows, rows), 1)
    same_chunk = jnp.right_shift(r_i, shift) == jnp.right_shift(c_i, shift)
    chunk_tri = jnp.logical_and(same_chunk, r_i >= c_i).astype(F32)
    causal = (lax.broadcasted_iota(jnp.int32, (cb, cb), 0)
              >= lax.broadcasted_iota(jnp.int32, (cb, cb), 1))[None]

    kd = heads * dk
    x = jnp.dot(la_ref[...], wa2_ref[...], precision=HIGHEST, preferred_element_type=F32) + ba_ref[...]
    log_a = (jnp.minimum(x, 0.0) - jnp.log1p(jnp.exp(-jnp.abs(x)))) * (1.0 / GLA_TAU)
    b = jnp.dot(chunk_tri, log_a, precision=HIGHEST, preferred_element_type=F32).reshape(nc, cb, kd)
    b_mid = b[:, cb // 2 - 1:cb // 2]
    b_last = b[:, cb - 1:cb]
    q = (lq_ref[...] * (dk ** -0.5)).reshape(nc, cb, kd)
    k = lk_ref[...].reshape(nc, cb, kd)
    q_dec_all = (q * jnp.exp(b)).astype(BF16)
    q_mid_all = (q * jnp.exp(b - b_mid)).astype(BF16)
    k_mid_all = (k * jnp.exp(b_mid - b)).astype(BF16)
    k_rem_all = (k * jnp.exp(b_last - b)).astype(BF16)
    decay_all = jnp.exp(b_last)
    pre = []
    for h in range(heads):
        ks = slice(h * dk, (h + 1) * dk)
        v = lv_ref[:, h * dv:(h + 1) * dv].astype(BF16).reshape(nc, cb, dv)
        att = jnp.einsum("clk,cmk->clm", q_mid_all[:, :, ks], k_mid_all[:, :, ks],
                         preferred_element_type=F32)
        att = jnp.where(causal, att, 0.0).astype(BF16)
        o_intra = jnp.einsum("clm,cmv->clv", att, v, preferred_element_type=F32)
        pre.append((q_dec_all[:, :, ks], k_rem_all[:, :, ks], v, o_intra, decay_all[:, :, ks]))

    for s in range(nseg):
        @pl.when(is_sample)
        def _():
            for h in range(heads):
                state_ref[h] = s0_ref[s, h].T

        if s == 0:
            @pl.when(t == 0)
            def _():
                state_ref[...] = jnp.zeros_like(state_ref)

        for c in range(s * chunks_per_seg, (s + 1) * chunks_per_seg):
            for h in range(heads):
                q_dec, k_rem, v, o_intra, decay = pre[h]
                state = state_ref[h]
                o = o_intra[c] + lax.dot_general(q_dec[c], state.astype(BF16), NT_DIMS,
                                                 preferred_element_type=F32)
                kv = lax.dot_general(v[c], k_rem[c], TN_DIMS, preferred_element_type=F32)
                state_ref[h] = decay[c] * state + kv
                o_ref[c * cb:(c + 1) * cb, h * dv:(h + 1) * dv] = o

        @pl.when(is_sample)
        def _():
            for h in range(heads):
                souts_ref[s, h] = state_ref[h].T

        if s == nseg - 1:
            @pl.when(t == n_prompt_tiles - 1)
            def _():
                for h in range(heads):
                    soutp_ref[0, h] = state_ref[h].T

    g = g_ref[...]
    for h in range(heads):
        vs = slice(h * dv, (h + 1) * dv)
        r = lr_ref[:, vs]
        y_ref[:, vs] = (_rms(o_ref[:, vs], g) * (r * jax.nn.sigmoid(r))).astype(y_ref.dtype)


def gla(z, la, w_a2, b_a, g_gla, s0, layer, n_prompt_rows, seg_rows, col_q, col_k, col_v, col_r, tr=128):
    m = z.shape[0]
    _, nseq, heads, dk, dv = s0.shape
    kd, vd = heads * dk, heads * dv
    npt = n_prompt_rows // tr
    spt = tr // seg_rows
    assert n_prompt_rows % tr == 0 and tr % seg_rows == 0 and seg_rows % GLA_CHUNK == 0
    sample_tile = lambda t: (jnp.maximum(t - npt, 0), 0, 0, 0)
    wa2p = jnp.zeros((LA_PAD, kd), F32).at[:w_a2.shape[0]].set(w_a2)
    return pl.pallas_call(
        functools.partial(_gla_body, npt, seg_rows), grid=(m // tr,),
        in_specs=[pl.BlockSpec((tr, kd), lambda t: (t, col_q // kd)),
                  pl.BlockSpec((tr, kd), lambda t: (t, col_k // kd)),
                  pl.BlockSpec((tr, vd), lambda t: (t, col_v // vd)),
                  pl.BlockSpec((tr, vd), lambda t: (t, col_r // vd)),
                  pl.BlockSpec((tr, LA_PAD), lambda t: (t, 0)),
                  pl.BlockSpec((LA_PAD, kd), lambda t: (0, 0)),
                  pl.BlockSpec((1, kd), lambda t: (0, 0)),
                  pl.BlockSpec((1, dv), lambda t: (0, 0)),
                  pl.BlockSpec((None, spt, heads, dk, dv), lambda t: (layer,) + sample_tile(t))],
        out_specs=[pl.BlockSpec((tr, vd), lambda t: (t, 0)),
                   pl.BlockSpec((1, heads, dk, dv), lambda t: (0, 0, 0, 0)),
                   pl.BlockSpec((spt, heads, dk, dv), sample_tile)],
        out_shape=[jax.ShapeDtypeStruct((m, vd), BF16),
                   jax.ShapeDtypeStruct((1, heads, dk, dv), F32),
                   jax.ShapeDtypeStruct((nseq, heads, dk, dv), F32)],
        scratch_shapes=[pltpu.VMEM((heads, dv, dk), F32), pltpu.VMEM((tr, vd), F32)],
        compiler_params=_params(("arbitrary",), VMEM_LIMIT), name="gla",
    )(z, z, z, z, la, wa2p, b_a.reshape(1, -1), g_gla.reshape(1, -1), s0)


def _branch_body(ya_ref, yb_ref, yc_ref, wa_ref, wb_ref, wc_ref, ga_ref, gb_ref, gc_ref, o_ref):
    m = jax.nn.sigmoid(ga_ref[...]) * jnp.dot(ya_ref[...], wa_ref[...], preferred_element_type=F32)
    m = m + jax.nn.sigmoid(gb_ref[...]) * jnp.dot(yb_ref[...], wb_ref[...], preferred_element_type=F32)
    m = m + jax.nn.sigmoid(gc_ref[...]) * jnp.dot(yc_ref[...], wc_ref[...], preferred_element_type=F32)
    o_ref[...] = m.astype(o_ref.dtype)


def branch_merge(ya, yb, yc, w_branch, layer, z, gate_col, tm=1024, tn=512):
    m = ya.shape[0]
    ca, cb, cc = ya.shape[1], yb.shape[1], yc.shape[1]
    d = w_branch.shape[2]
    assert ca == cb and cc % ca == 0 and (ca + cb) % cc == 0
    gb = gate_col // tn
    nd = d // tn
    return pl.pallas_call(
        _branch_body, grid=(m // tm, nd),
        in_specs=[pl.BlockSpec((tm, ca), lambda i, j: (i, 0)),
                  pl.BlockSpec((tm, cb), lambda i, j: (i, 0)),
                  pl.BlockSpec((tm, cc), lambda i, j: (i, 0)),
                  pl.BlockSpec((None, ca, tn), lambda i, j: (layer, 0, j)),
                  pl.BlockSpec((None, cb, tn), lambda i, j: (layer, 1, j)),
                  pl.BlockSpec((None, cc, tn), lambda i, j: (layer, (ca + cb) // cc, j)),
                  pl.BlockSpec((tm, tn), lambda i, j: (i, gb + j)),
                  pl.BlockSpec((tm, tn), lambda i, j: (i, gb + nd + j)),
                  pl.BlockSpec((tm, tn), lambda i, j: (i, gb + 2 * nd + j))],
        out_specs=pl.BlockSpec((tm, tn), lambda i, j: (i, j)),
        out_shape=jax.ShapeDtypeStruct((m, d), BF16),
        compiler_params=_params(("parallel", "parallel"), VMEM_LIMIT), name="branch_merge",
    )(ya, yb, yc, w_branch, w_branch, w_branch, z, z, z)


def kernel(x_prompt, x_sample, cache_conv, cache_k, cache_v, state_gla, g_mix, w_in, conv_w, g_q, g_k,
           rel_bias, w_a2, b_a, g_gla, w_branch, w_out, g_ffn, w_gu, w_down):
    nb, seq, d = x_prompt.shape
    nd, ds, _ = x_sample.shape
    depth = g_mix.shape[0]
    assert nb == 1 and ds == CHUNK
    mp = nb * seq
    ms = nd * ds
    conv_dim = conv_w.shape[2]
    att_dim = ATT_HEADS * ATT_HEAD_DIM
    kdim = GLA_HEADS * GLA_DK
    vdim = GLA_HEADS * GLA_DV
    col_att = 3 * conv_dim
    col_lq = col_att + 3 * att_dim
    col_lk = col_lq + kdim
    col_lv = col_lk + kdim
    col_lr = col_lv + vdim
    col_la = col_lr + vdim
    col_gate = col_la
    keep = min(WINDOW, seq)

    xp = x_prompt.reshape(mp, d)
    xs = x_sample.reshape(ms, d)
    w_branch_b = w_branch.astype(BF16)
    w_out_b = w_out.astype(BF16)
    w_down_b = w_down.astype(BF16)
    w_in_t = jnp.swapaxes(w_in, 1, 2)
    outs = {name: [] for name in ("conv_p", "k_p", "v_p", "gla_p", "conv_s", "k_s", "v_s", "gla_s")}
    for l in range(depth):
        h, la = rmsnorm_la(xp, xs, g_mix[l], w_in_t, l, col_la, GLA_RANK)
        z = in_proj(h, w_in_t, l, col_la, GLA_RANK)

        y_a, conv_tail = gated_conv(z, cache_conv[l], conv_w[l], mp)

        qn, kn, keys, vals = qk_norm(z, g_q[l], g_k[l], col_att // att_dim, ATT_HEADS)
        y_b = band_attention(qn, keys, vals, cache_k.reshape(depth, nd * WINDOW, att_dim),
                             cache_v.reshape(depth, nd * WINDOW, att_dim), l, rel_bias[l], mp)

        y_c, s_prompt, s_sample = gla(z, la, w_a2[l], b_a[l], g_gla[l], state_gla, l, mp, ds,
                                      col_lq, col_lk, col_lv, col_lr)

        mrg = branch_merge(y_a, y_b, y_c, w_branch_b, l, z, col_gate)
        xp, xs = matmul_residual(mrg, w_out_b, l, xp, xs)

        hf = rmsnorm_bf16(xp, xs, g_ffn[l])
        hid = swiglu_up(hf, w_gu, l)
        xp, xs = matmul_residual(hid, w_down_b, l, xp, xs, tm=512, tn=512)

        v_new = z[:, col_att + 2 * att_dim:col_att + 3 * att_dim]
        outs["conv_p"].append(conv_tail[mp // CHUNK - 1, 6:8][None])
        outs["conv_s"].append(conv_tail[mp // CHUNK:, 6:8])
        outs["k_p"].append(kn[mp - keep:mp].reshape(nb, keep, ATT_HEADS, ATT_HEAD_DIM))
        outs["v_p"].append(v_new[mp - keep:mp].reshape(nb, keep, ATT_HEADS, ATT_HEAD_DIM))
        outs["k_s"].append(kn[mp:].reshape(nd, ds, ATT_HEADS, ATT_HEAD_DIM))
        outs["v_s"].append(v_new[mp:].reshape(nd, ds, ATT_HEADS, ATT_HEAD_DIM))
        outs["gla_p"].append(s_prompt)
        outs["gla_s"].append(s_sample)

    st = {name: jnp.stack(v) for name, v in outs.items()}
    return (xp.reshape(nb, seq, d), xs.reshape(nd, ds, d),
            st["conv_p"], st["k_p"], st["v_p"], st["gla_p"],
            st["conv_s"], st["k_s"], st["v_s"], st["gla_s"])
```

```python
import functools

import jax
import jax.numpy as jnp
from jax import lax
from jax.experimental import pallas as pl
from jax.experimental.pallas import tpu as pltpu

F32 = jnp.float32
BF16 = jnp.bfloat16
HIGHEST = lax.Precision.HIGHEST

CHUNK = 64
BAND_CHUNKS = 8
WINDOW = BAND_CHUNKS * CHUNK
SPAN = WINDOW + CHUNK
REL_CLIP = 128
ATT_HEADS = 8
ATT_HEAD_DIM = 128
GLA_HEADS = 4
GLA_DK = 256
GLA_DV = 512
GLA_RANK = 16
GLA_TAU = 16.0
GLA_CHUNK = 32
LA_PAD = 128
PROMPT_ROW_GROUPS = 2
NEG_INF = -1e30
EPS = 1e-6
VMEM_LIMIT = 56 * 1024 * 1024

NT_DIMS = (((1,), (1,)), ((), ()))
TN_DIMS = (((0,), (0,)), ((), ()))


def _params(sem, vmem=None):
    return pltpu.CompilerParams(dimension_semantics=sem, vmem_limit_bytes=vmem)


def _rms(x, g):
    return x * lax.rsqrt(jnp.mean(x * x, axis=-1, keepdims=True) + EPS) * g


def _split_row_specs(tr, tn, n_prompt_tiles, row_axis, col_of):
    def prompt_map(*ids):
        return (jnp.minimum(ids[row_axis], n_prompt_tiles - 1), col_of(*ids))

    def sample_map(*ids):
        return (jnp.maximum(ids[row_axis] - n_prompt_tiles, 0), col_of(*ids))

    return pl.BlockSpec((tr, tn), prompt_map), pl.BlockSpec((tr, tn), sample_map)


def _pick_rows(i, n_prompt_tiles, p_ref, s_ref):
    return jnp.where(i < n_prompt_tiles, p_ref[...], s_ref[...])


def _norm_body(n_prompt_tiles, xp_ref, xs_ref, g_ref, h_ref):
    x = _pick_rows(pl.program_id(0), n_prompt_tiles, xp_ref, xs_ref)
    h_ref[...] = _rms(x, g_ref[...]).astype(h_ref.dtype)


def _norm_la_body(n_prompt_tiles, xp_ref, xs_ref, g_ref, wla_ref, h_ref, la_ref, wlab_ref):
    rank = wla_ref.shape[0]

    @pl.when(pl.program_id(0) == 0)
    def _():
        wlab_ref[...] = jnp.zeros_like(wlab_ref)
        wlab_ref[0:rank, :] = wla_ref[...].astype(wlab_ref.dtype)

    x = _pick_rows(pl.program_id(0), n_prompt_tiles, xp_ref, xs_ref)
    h = _rms(x, g_ref[...]).astype(h_ref.dtype)
    h_ref[...] = h
    la_ref[...] = lax.dot_general(h, wlab_ref[...], NT_DIMS, preferred_element_type=F32)


def rmsnorm_bf16(xp, xs, g, tr=512):
    (mp, d), ms = xp.shape, xs.shape[0]
    assert mp % tr == 0 and ms % tr == 0
    p_spec, s_spec = _split_row_specs(tr, d, mp // tr, 0, lambda i: 0)
    return pl.pallas_call(
        functools.partial(_norm_body, mp // tr), grid=((mp + ms) // tr,),
        in_specs=[p_spec, s_spec, pl.BlockSpec((1, d), lambda i: (0, 0))],
        out_specs=pl.BlockSpec((tr, d), lambda i: (i, 0)),
        out_shape=jax.ShapeDtypeStruct((mp + ms, d), BF16),
        compiler_params=_params(("parallel",), VMEM_LIMIT), name="rmsnorm",
    )(xp, xs, g.reshape(1, d))


def rmsnorm_la(xp, xs, g, w_t, layer, la_row, rank, tr=512):
    (mp, d), ms = xp.shape, xs.shape[0]
    assert mp % tr == 0 and ms % tr == 0
    m = mp + ms
    p_spec, s_spec = _split_row_specs(tr, d, mp // tr, 0, lambda i: 0)
    return pl.pallas_call(
        functools.partial(_norm_la_body, mp // tr), grid=(m // tr,),
        in_specs=[p_spec, s_spec, pl.BlockSpec((1, d), lambda i: (0, 0)),
                  pl.BlockSpec((None, rank, d), lambda i: (layer, la_row // rank, 0))],
        out_specs=[pl.BlockSpec((tr, d), lambda i: (i, 0)), pl.BlockSpec((tr, LA_PAD), lambda i: (i, 0))],
        out_shape=[jax.ShapeDtypeStruct((m, d), BF16), jax.ShapeDtypeStruct((m, LA_PAD), F32)],
        scratch_shapes=[pltpu.VMEM((LA_PAD, d), BF16)],
        compiler_params=_params(("arbitrary",), VMEM_LIMIT), name="rmsnorm_la",
    )(xp, xs, g.reshape(1, d), w_t)


CAST_ROWS = 512


def _in_proj_body(layer, n_aligned, skip, a_ref, w_hbm, o_ref, wf_ref, wb_ref, sem):
    j = pl.program_id(0)
    i = pl.program_id(1)
    tn = wb_ref.shape[0]

    def weight_copy(jj):
        row0 = pl.multiple_of(jj * tn + jnp.where(jj >= n_aligned, skip, 0), 16)
        return pltpu.make_async_copy(w_hbm.at[layer, pl.ds(row0, tn), :], wf_ref, sem)

    @pl.when(jnp.logical_and(j == 0, i == 0))
    def _():
        weight_copy(0).start()

    @pl.when(i == 0)
    def _():
        weight_copy(j).wait()
        for r in range(0, tn, 128):
            wb_ref[r:r + 128, :] = wf_ref[r:r + 128, :].astype(wb_ref.dtype)

        @pl.when(j + 1 < pl.num_programs(0))
        def _():
            weight_copy(j + 1).start()

    o_ref[...] = lax.dot_general(a_ref[...], wb_ref[...], NT_DIMS, preferred_element_type=F32)


def in_proj(a, w_t, layer, skip_row, skip, tm=1024, tn=1024):
    m, k = a.shape
    n = w_t.shape[1] - skip
    assert skip_row % tn == 0 and n % tn == 0 and skip % 16 == 0
    return pl.pallas_call(
        functools.partial(_in_proj_body, layer, skip_row // tn, skip), grid=(n // tn, m // tm),
        in_specs=[pl.BlockSpec((tm, k), lambda j, i: (i, 0)),
                  pl.BlockSpec(memory_space=pl.ANY)],
        out_specs=pl.BlockSpec((tm, tn), lambda j, i: (i, j)),
        out_shape=jax.ShapeDtypeStruct((m, n), F32),
        scratch_shapes=[pltpu.VMEM((tn, k), F32), pltpu.VMEM((tn, k), BF16), pltpu.SemaphoreType.DMA(())],
        compiler_params=_params(("arbitrary", "arbitrary"), VMEM_LIMIT), name="in_proj",
    )(a, w_t)


def _mm_res_body(n_prompt_tiles, a_ref, b_ref, rp_ref, rs_ref, op_ref, os_ref):
    i = pl.program_id(1)
    acc = jnp.dot(a_ref[...], b_ref[...], preferred_element_type=F32)

    @pl.when(i < n_prompt_tiles)
    def _():
        op_ref[...] = rp_ref[...] + acc

    @pl.when(i >= n_prompt_tiles)
    def _():
        os_ref[...] = rs_ref[...] + acc


def matmul_residual(a, b, layer, res_p, res_s, tm=1024, tn=512):
    m, k = a.shape
    n = b.shape[2]
    mp, ms = res_p.shape[0], res_s.shape[0]
    assert mp % tm == 0 and ms % tm == 0 and mp + ms == m
    p_spec, s_spec = _split_row_specs(tm, tn, mp // tm, 1, lambda j, i: j)
    return pl.pallas_call(
        functools.partial(_mm_res_body, mp // tm), grid=(n // tn, m // tm),
        in_specs=[pl.BlockSpec((tm, k), lambda j, i: (i, 0)),
                  pl.BlockSpec((None, k, tn), lambda j, i: (layer, 0, j)),
                  p_spec, s_spec],
        out_specs=[p_spec, s_spec],
        out_shape=[jax.ShapeDtypeStruct((mp, n), F32), jax.ShapeDtypeStruct((ms, n), F32)],
        compiler_params=_params(("parallel", "arbitrary"), VMEM_LIMIT), name="matmul_residual",
    )(a, b, res_p, res_s)


def _swiglu_body(a_ref, wg_ref, wu_ref, o_ref, wb_ref):
    tn = wg_ref.shape[1]

    @pl.when(pl.program_id(1) == 0)
    def _():
        def step(r, carry):
            rs = pl.ds(pl.multiple_of(r * CAST_ROWS, CAST_ROWS), CAST_ROWS)
            wb_ref[rs, 0:tn] = wg_ref[rs, :].astype(wb_ref.dtype)
            wb_ref[rs, tn:2 * tn] = wu_ref[rs, :].astype(wb_ref.dtype)
            return carry
        lax.fori_loop(0, wg_ref.shape[0] // CAST_ROWS, step, 0)

    u = jnp.dot(a_ref[...], wb_ref[...], preferred_element_type=F32)
    ug = u[:, 0:tn]
    o_ref[...] = (ug * jax.nn.sigmoid(ug) * u[:, tn:2 * tn]).astype(o_ref.dtype)


def swiglu_up(a, w_gu, layer, tm=1536, tn=256):
    m, k = a.shape
    dff = w_gu.shape[2] // 2
    nb = dff // tn
    return pl.pallas_call(
        _swiglu_body, grid=(nb, m // tm),
        in_specs=[pl.BlockSpec((tm, k), lambda j, i: (i, 0)),
                  pl.BlockSpec((None, k, tn), lambda j, i: (layer, 0, j)),
                  pl.BlockSpec((None, k, tn), lambda j, i: (layer, 0, j + nb))],
        out_specs=pl.BlockSpec((tm, tn), lambda j, i: (i, j)),
        out_shape=jax.ShapeDtypeStruct((m, dff), BF16),
        scratch_shapes=[pltpu.VMEM((k, 2 * tn), BF16)],
        compiler_params=_params(("parallel", "arbitrary"), VMEM_LIMIT), name="swiglu_up",
    )(a, w_gu, w_gu)


def _conv_body(n_prompt_tiles, group, cb_ref, cc_ref, cx_ref, pc_ref, px_ref, cache_ref, w_ref,
               y_ref, tail_ref):
    t = pl.program_id(0)
    u = cc_ref[...] * cx_ref[...]
    rows = u.shape[0]
    row = lax.broadcasted_iota(jnp.int32, u.shape, 0)
    u1 = pltpu.roll(u, 1, 0)
    u2 = pltpu.roll(u, 2, 0)
    w = w_ref[...]

    def finish(um1, um2):
        y = um2 * w[0:1] + um1 * w[1:2] + u * w[2:3]
        y_ref[...] = (cb_ref[...] * y).astype(y_ref.dtype)

    @pl.when(t < n_prompt_tiles)
    def _():
        prev = jnp.where(t > 0, pc_ref[...] * px_ref[...], 0.0)
        finish(jnp.where(row == 0, prev[7:8], u1),
               jnp.where(row == 0, prev[6:7], jnp.where(row == 1, prev[7:8], u2)))

    @pl.when(t >= n_prompt_tiles)
    def _():
        um1, um2 = u1, u2
        for g in range(rows // group):
            c0 = cache_ref[g, 0:1]
            c1 = cache_ref[g, 1:2]
            um1 = jnp.where(row == g * group, c1, um1)
            um2 = jnp.where(row == g * group, c0, jnp.where(row == g * group + 1, c1, um2))
        finish(um1, um2)

    for g in range(rows // group):
        tail_ref[g] = u[(g + 1) * group - 8:(g + 1) * group]


def gated_conv(z, cache, w, n_prompt_rows, group=CHUNK, tr=512):
    m = z.shape[0]
    c = w.shape[1]
    gpt = tr // group
    npt = n_prompt_rows // tr
    assert n_prompt_rows % tr == 0 and m % tr == 0 and tr % group == 0
    row_spec = lambda col: pl.BlockSpec((tr, c), lambda t: (t, col))
    prev_spec = lambda col: pl.BlockSpec((8, c), lambda t: (jnp.maximum(t * (tr // 8) - 1, 0), col))
    return pl.pallas_call(
        functools.partial(_conv_body, npt, group), grid=(m // tr,),
        in_specs=[row_spec(0), row_spec(1), row_spec(2), prev_spec(1), prev_spec(2),
                  pl.BlockSpec((gpt, 2, c), lambda t: (jnp.maximum(t - npt, 0), 0, 0)),
                  pl.BlockSpec((3, c), lambda t: (0, 0))],
        out_specs=[pl.BlockSpec((tr, c), lambda t: (t, 0)),
                   pl.BlockSpec((gpt, 8, c), lambda t: (t, 0, 0))],
        out_shape=[jax.ShapeDtypeStruct((m, c), BF16), jax.ShapeDtypeStruct((m // group, 8, c), F32)],
        compiler_params=_params(("parallel",), VMEM_LIMIT), name="gated_conv",
    )(z, z, z, z, z, cache, w)


def _qk_body(heads, aq_ref, ak_ref, av_ref, gq_ref, gk_ref, qn_ref, kn_ref, kpad_ref, vpad_ref):
    s = pl.program_id(0)

    @pl.when(s == 0)
    def _():
        kpad_ref[...] = jnp.zeros_like(kpad_ref)
        vpad_ref[...] = jnp.zeros_like(vpad_ref)

    @pl.when(s > 0)
    def _():
        dh = gq_ref.shape[1]
        gq = gq_ref[...]
        gk = gk_ref[...]
        for h in range(heads):
            sl = slice(h * dh, (h + 1) * dh)
            qn_ref[:, sl] = _rms(aq_ref[:, sl], gq).astype(qn_ref.dtype)
            kn = _rms(ak_ref[:, sl], gk)
            kn_ref[:, sl] = kn
            kpad_ref[:, sl] = kn.astype(kpad_ref.dtype)
        vpad_ref[...] = av_ref[...].astype(vpad_ref.dtype)


def qk_norm(z, g_q, g_k, col0, heads, tr=WINDOW):
    m = z.shape[0]
    dh = g_q.shape[0]
    c = heads * dh
    assert tr == WINDOW and m % tr == 0
    tile = lambda s: jnp.maximum(s - 1, 0)
    spec = lambda col: pl.BlockSpec((tr, c), lambda s: (tile(s), col))
    g_spec = pl.BlockSpec((1, dh), lambda s: (0, 0))
    o_spec = pl.BlockSpec((tr, c), lambda s: (tile(s), 0))
    pad_spec = pl.BlockSpec((tr, c), lambda s: (s, 0))
    return pl.pallas_call(
        functools.partial(_qk_body, heads), grid=(1 + m // tr,),
        in_specs=[spec(col0), spec(col0 + 1), spec(col0 + 2), g_spec, g_spec],
        out_specs=[o_spec, o_spec, pad_spec, pad_spec],
        out_shape=[jax.ShapeDtypeStruct((m, c), BF16), jax.ShapeDtypeStruct((m, c), F32),
                   jax.ShapeDtypeStruct((WINDOW + m, c), BF16), jax.ShapeDtypeStruct((WINDOW + m, c), BF16)],
        compiler_params=_params(("arbitrary",), VMEM_LIMIT), name="qk_norm",
    )(z, z, z, g_q.reshape(1, dh), g_k.reshape(1, dh))


def _toeplitz_bias(vec, n_queries, n_keys):
    lanes = vec.shape[-1]
    return pltpu.roll(jnp.broadcast_to(vec, (n_queries, lanes)), 0, 1, stride=1, stride_axis=0)[:, :n_keys]


def _softmax_pv(s, v):
    p = jnp.exp(s - jnp.max(s, axis=-1, keepdims=True))
    p = p * (1.0 / jnp.sum(p, axis=-1, keepdims=True))
    return jnp.dot(p.astype(v.dtype), v, preferred_element_type=F32)


def _attn_body(n_prompt_tiles, chunks_per_tile, prompt_key_rows, scale,
               q_ref, k_ref, v_ref, ck_ref, cv_ref, bias_ref, tbias_ref, o_ref, tile_bias_ref):
    t = pl.program_id(1)
    tr = chunks_per_tile * CHUNK
    tk = tr + WINDOW

    @pl.when(t == 0)
    def _():
        row = lax.broadcasted_iota(jnp.int32, (tr, tk), 0)
        col = lax.broadcasted_iota(jnp.int32, (tr, tk), 1)
        first = jnp.left_shift(jnp.right_shift(row, CHUNK.bit_length() - 1), CHUNK.bit_length() - 1)
        in_band = jnp.logical_and(col >= first, col < first + SPAN)
        tile_bias_ref[...] = jnp.where(in_band, _toeplitz_bias(tbias_ref[0], tr, tk), NEG_INF)

    @pl.when(t < n_prompt_tiles)
    def _():
        start = pl.multiple_of(t * tr, tr)
        k = k_ref[pl.ds(start, tk), :]
        v = v_ref[pl.ds(start, tk), :]
        rg = tr // PROMPT_ROW_GROUPS
        col = lax.broadcasted_iota(jnp.int32, (rg, tk), 1)
        for r in range(0, tr, rg):
            s = lax.dot_general(q_ref[r:r + rg, :], k, NT_DIMS, preferred_element_type=F32)
            s = s * scale + tile_bias_ref[r:r + rg, :]
            s = jnp.where(col >= WINDOW - t * tr, s, NEG_INF)
            o_ref[r:r + rg, :] = _softmax_pv(s, v).astype(o_ref.dtype)

    @pl.when(t >= n_prompt_tiles)
    def _():
        bias = _toeplitz_bias(bias_ref[0], CHUNK, SPAN)
        for c in range(chunks_per_tile):
            n = (t - n_prompt_tiles) * chunks_per_tile + c
            old = pl.ds(pl.multiple_of(n * WINDOW, WINDOW), WINDOW)
            new = pl.ds(pl.multiple_of(prompt_key_rows + n * CHUNK, CHUNK), CHUNK)
            q = q_ref[c * CHUNK:(c + 1) * CHUNK, :]
            k = jnp.concatenate([ck_ref[old, :].astype(k_ref.dtype), k_ref[new, :]], axis=0)
            v = jnp.concatenate([cv_ref[old, :].astype(v_ref.dtype), v_ref[new, :]], axis=0)
            s = lax.dot_general(q, k, NT_DIMS, preferred_element_type=F32) * scale + bias
            o_ref[c * CHUNK:(c + 1) * CHUNK, :] = _softmax_pv(s, v).astype(o_ref.dtype)


def band_attention(qn, keys, vals, cache_k, cache_v, layer, rel_bias, n_prompt_rows, chunks_per_tile=8):
    m, c = qn.shape
    heads = rel_bias.shape[0]
    dh = c // heads
    tr = CHUNK * chunks_per_tile
    tk = tr + WINDOW
    kr = keys.shape[0]
    cr = cache_k.shape[1]
    assert n_prompt_rows % tr == 0 and (m - n_prompt_rows) % tr == 0
    assert kr == WINDOW + m and cr == (m - n_prompt_rows) // CHUNK * WINDOW
    body = functools.partial(_attn_body, n_prompt_rows // tr, chunks_per_tile,
                             n_prompt_rows + WINDOW, dh ** -0.5)
    return pl.pallas_call(
        body, grid=(heads, m // tr),
        in_specs=[pl.BlockSpec((tr, dh), lambda h, t: (t, h)),
                  pl.BlockSpec((kr, dh), lambda h, t: (0, h)),
                  pl.BlockSpec((kr, dh), lambda h, t: (0, h)),
                  pl.BlockSpec((None, cr, dh), lambda h, t: (layer, 0, h)),
                  pl.BlockSpec((None, cr, dh), lambda h, t: (layer, 0, h)),
                  pl.BlockSpec((1, 1, SPAN + CHUNK), lambda h, t: (h, 0, 0)),
                  pl.BlockSpec((1, 1, tk + tr), lambda h, t: (h, 0, 0))],
        out_specs=pl.BlockSpec((tr, dh), lambda h, t: (t, h)),
        out_shape=jax.ShapeDtypeStruct((m, c), BF16),
        scratch_shapes=[pltpu.VMEM((tr, tk), F32)],
        compiler_params=_params(("parallel", "arbitrary"), VMEM_LIMIT), name="band_attention",
    )(qn, keys, vals, cache_k, cache_v,
      rel_bias_rows(rel_bias, CHUNK, SPAN), rel_bias_rows(rel_bias, tr, tk))


def rel_bias_rows(rel_bias, n_queries, n_keys):
    lanes = n_keys + n_queries
    i = jnp.arange(lanes)
    dist = jnp.where(i < n_keys, -i, lanes - i)
    rel = jnp.clip(dist + WINDOW, -REL_CLIP, REL_CLIP) + REL_CLIP
    return rel_bias.astype(F32)[:, rel].reshape(rel_bias.shape[0], 1, lanes)


def _gla_body(n_prompt_tiles, seg_rows, lq_ref, lk_ref, lv_ref, lr_ref, la_ref, wa2_ref, ba_ref, g_ref,
              s0_ref, y_ref, soutp_ref, souts_ref, state_ref, o_ref):
    t = pl.program_id(0)
    rows = lq_ref.shape[0]
    heads, dv, dk = state_ref.shape
    cb = GLA_CHUNK
    nc = rows // cb
    chunks_per_seg = seg_rows // cb
    nseg = rows // seg_rows
    is_sample = t >= n_prompt_tiles
    shift = cb.bit_length() - 1

    r_i = lax.broadcasted_iota(jnp.int32, (rows, rows), 0)
    c_i = lax.broadcasted_iota(jnp.int32, (rows, rows), 1)
    same_chunk = jnp.right_shift(r_i, shift) == jnp.right_shift(c_i, shift)
    chunk_tri = jnp.logical_and(same_chunk, r_i >= c_i).astype(F32)
    causal = (lax.broadcasted_iota(jnp.int32, (cb, cb), 0)
              >= lax.broadcasted_iota(jnp.int32, (cb, cb), 1))[None]

    kd = heads * dk
    x = jnp.dot(la_ref[...], wa2_ref[...], precision=HIGHEST, preferred_element_type=F32) + ba_ref[...]
    log_a = (jnp.minimum(x, 0.0) - jnp.log1p(jnp.exp(-jnp.abs(x)))) * (1.0 / GLA_TAU)
    b = jnp.dot(chunk_tri, log_a, precision=HIGHEST, preferred_element_type=F32).reshape(nc, cb, kd)
    b_mid = b[:, cb // 2 - 1:cb // 2]
    b_last = b[:, cb - 1:cb]
    q = (lq_ref[...] * (dk ** -0.5)).reshape(nc, cb, kd)
    k = lk_ref[...].reshape(nc, cb, kd)
    q_dec_all = (q * jnp.exp(b)).astype(BF16)
    q_mid_all = (q * jnp.exp(b - b_mid)).astype(BF16)
    k_mid_all = (k * jnp.exp(b_mid - b)).astype(BF16)
    k_rem_all = (k * jnp.exp(b_last - b)).astype(BF16)
    decay_all = jnp.exp(b_last)
    pre = []
    for h in range(heads):
        ks = slice(h * dk, (h + 1) * dk)
        v = lv_ref[:, h * dv:(h + 1) * dv].astype(BF16).reshape(nc, cb, dv)
        att = jnp.einsum("clk,cmk->clm", q_mid_all[:, :, ks], k_mid_all[:, :, ks],
                         preferred_element_type=F32)
        att = jnp.where(causal, att, 0.0).astype(BF16)
        o_intra = jnp.einsum("clm,cmv->clv", att, v, preferred_element_type=F32)
        pre.append((q_dec_all[:, :, ks], k_rem_all[:, :, ks], v, o_intra, decay_all[:, :, ks]))

    for s in range(nseg):
        @pl.when(is_sample)
        def _():
            for h in range(heads):
                state_ref[h] = s0_ref[s, h].T

        if s == 0:
            @pl.when(t == 0)
            def _():
                state_ref[...] = jnp.zeros_like(state_ref)

        for c in range(s * chunks_per_seg, (s + 1) * chunks_per_seg):
            for h in range(heads):
                q_dec, k_rem, v, o_intra, decay = pre[h]
                state = state_ref[h]
                o = o_intra[c] + lax.dot_general(q_dec[c], state.astype(BF16), NT_DIMS,
                                                 preferred_element_type=F32)
                kv = lax.dot_general(v[c], k_rem[c], TN_DIMS, preferred_element_type=F32)
                state_ref[h] = decay[c] * state + kv
                o_ref[c * cb:(c + 1) * cb, h * dv:(h + 1) * dv] = o

        @pl.when(is_sample)
        def _():
            for h in range(heads):
                souts_ref[s, h] = state_ref[h].T

        if s == nseg - 1:
            @pl.when(t == n_prompt_tiles - 1)
            def _():
                for h in range(heads):
                    soutp_ref[0, h] = state_ref[h].T

    g = g_ref[...]
    for h in range(heads):
        vs = slice(h * dv, (h + 1) * dv)
        r = lr_ref[:, vs]
        y_ref[:, vs] = (_rms(o_ref[:, vs], g) * (r * jax.nn.sigmoid(r))).astype(y_ref.dtype)


def gla(z, la, w_a2, b_a, g_gla, s0, layer, n_prompt_rows, seg_rows, col_q, col_k, col_v, col_r, tr=128):
    m = z.shape[0]
    _, nseq, heads, dk, dv = s0.shape
    kd, vd = heads * dk, heads * dv
    npt = n_prompt_rows // tr
    spt = tr // seg_rows
    assert n_prompt_rows % tr == 0 and tr % seg_rows == 0 and seg_rows % GLA_CHUNK == 0
    sample_tile = lambda t: (jnp.maximum(t - npt, 0), 0, 0, 0)
    wa2p = jnp.zeros((LA_PAD, kd), F32).at[:w_a2.shape[0]].set(w_a2)
    return pl.pallas_call(
        functools.partial(_gla_body, npt, seg_rows), grid=(m // tr,),
        in_specs=[pl.BlockSpec((tr, kd), lambda t: (t, col_q // kd)),
                  pl.BlockSpec((tr, kd), lambda t: (t, col_k // kd)),
                  pl.BlockSpec((tr, vd), lambda t: (t, col_v // vd)),
                  pl.BlockSpec((tr, vd), lambda t: (t, col_r // vd)),
                  pl.BlockSpec((tr, LA_PAD), lambda t: (t, 0)),
                  pl.BlockSpec((LA_PAD, kd), lambda t: (0, 0)),
                  pl.BlockSpec((1, kd), lambda t: (0, 0)),
                  pl.BlockSpec((1, dv), lambda t: (0, 0)),
                  pl.BlockSpec((None, spt, heads, dk, dv), lambda t: (layer,) + sample_tile(t))],
        out_specs=[pl.BlockSpec((tr, vd), lambda t: (t, 0)),
                   pl.BlockSpec((1, heads, dk, dv), lambda t: (0, 0, 0, 0)),
                   pl.BlockSpec((spt, heads, dk, dv), sample_tile)],
        out_shape=[jax.ShapeDtypeStruct((m, vd), BF16),
                   jax.ShapeDtypeStruct((1, heads, dk, dv), F32),
                   jax.ShapeDtypeStruct((nseq, heads, dk, dv), F32)],
        scratch_shapes=[pltpu.VMEM((heads, dv, dk), F32), pltpu.VMEM((tr, vd), F32)],
        compiler_params=_params(("arbitrary",), VMEM_LIMIT), name="gla",
    )(z, z, z, z, la, wa2p, b_a.reshape(1, -1), g_gla.reshape(1, -1), s0)


def _branch_body(ya_ref, yb_ref, yc_ref, wa_ref, wb_ref, wc_ref, ga_ref, gb_ref, gc_ref, o_ref):
    m = jax.nn.sigmoid(ga_ref[...]) * jnp.dot(ya_ref[...], wa_ref[...], preferred_element_type=F32)
    m = m + jax.nn.sigmoid(gb_ref[...]) * jnp.dot(yb_ref[...], wb_ref[...], preferred_element_type=F32)
    m = m + jax.nn.sigmoid(gc_ref[...]) * jnp.dot(yc_ref[...], wc_ref[...], preferred_element_type=F32)
    o_ref[...] = m.astype(o_ref.dtype)


def branch_merge(ya, yb, yc, w_branch, layer, z, gate_col, tm=1024, tn=512):
    m = ya.shape[0]
    ca, cb, cc = ya.shape[1], yb.shape[1], yc.shape[1]
    d = w_branch.shape[2]
    assert ca == cb and cc % ca == 0 and (ca + cb) % cc == 0
    gb = gate_col // tn
    nd = d // tn
    return pl.pallas_call(
        _branch_body, grid=(m // tm, nd),
        in_specs=[pl.BlockSpec((tm, ca), lambda i, j: (i, 0)),
                  pl.BlockSpec((tm, cb), lambda i, j: (i, 0)),
                  pl.BlockSpec((tm, cc), lambda i, j: (i, 0)),
                  pl.BlockSpec((None, ca, tn), lambda i, j: (layer, 0, j)),
                  pl.BlockSpec((None, cb, tn), lambda i, j: (layer, 1, j)),
                  pl.BlockSpec((None, cc, tn), lambda i, j: (layer, (ca + cb) // cc, j)),
                  pl.BlockSpec((tm, tn), lambda i, j: (i, gb + j)),
                  pl.BlockSpec((tm, tn), lambda i, j: (i, gb + nd + j)),
                  pl.BlockSpec((tm, tn), lambda i, j: (i, gb + 2 * nd + j))],
        out_specs=pl.BlockSpec((tm, tn), lambda i, j: (i, j)),
        out_shape=jax.ShapeDtypeStruct((m, d), BF16),
        compiler_params=_params(("parallel", "parallel"), VMEM_LIMIT), name="branch_merge",
    )(ya, yb, yc, w_branch, w_branch, w_branch, z, z, z)


def kernel(x_prompt, x_sample, cache_conv, cache_k, cache_v, state_gla, g_mix, w_in, conv_w, g_q, g_k,
           rel_bias, w_a2, b_a, g_gla, w_branch, w_out, g_ffn, w_gu, w_down):
    nb, seq, d = x_prompt.shape
    nd, ds, _ = x_sample.shape
    depth = g_mix.shape[0]
    assert nb == 1 and ds == CHUNK
    mp = nb * seq
    ms = nd * ds
    conv_dim = conv_w.shape[2]
    att_dim = ATT_HEADS * ATT_HEAD_DIM
    kdim = GLA_HEADS * GLA_DK
    vdim = GLA_HEADS * GLA_DV
    col_att = 3 * conv_dim
    col_lq = col_att + 3 * att_dim
    col_lk = col_lq + kdim
    col_lv = col_lk + kdim
    col_lr = col_lv + vdim
    col_la = col_lr + vdim
    col_gate = col_la
    keep = min(WINDOW, seq)

    xp = x_prompt.reshape(mp, d)
    xs = x_sample.reshape(ms, d)
    w_branch_b = w_branch.astype(BF16)
    w_out_b = w_out.astype(BF16)
    w_down_b = w_down.astype(BF16)
    w_in_t = jnp.swapaxes(w_in, 1, 2)
    outs = {name: [] for name in ("conv_p", "k_p", "v_p", "gla_p", "conv_s", "k_s", "v_s", "gla_s")}
    for l in range(depth):
        h, la = rmsnorm_la(xp, xs, g_mix[l], w_in_t, l, col_la, GLA_RANK)
        z = in_proj(h, w_in_t, l, col_la, GLA_RANK)

        y_a, conv_tail = gated_conv(z, cache_conv[l], conv_w[l], mp)

        qn, kn, keys, vals = qk_norm(z, g_q[l], g_k[l], col_att // att_dim, ATT_HEADS)
        y_b = band_attention(qn, keys, vals, cache_k.reshape(depth, nd * WINDOW, att_dim),
                             cache_v.reshape(depth, nd * WINDOW, att_dim), l, rel_bias[l], mp)

        y_c, s_prompt, s_sample = gla(z, la, w_a2[l], b_a[l], g_gla[l], state_gla, l, mp, ds,
                                      col_lq, col_lk, col_lv, col_lr)

        mrg = branch_merge(y_a, y_b, y_c, w_branch_b, l, z, col_gate)
        xp, xs = matmul_residual(mrg, w_out_b, l, xp, xs)

        hf = rmsnorm_bf16(xp, xs, g_ffn[l])
        hid = swiglu_up(hf, w_gu, l)
        xp, xs = matmul_residual(hid, w_down_b, l, xp, xs, tm=512, tn=512)

        v_new = z[:, col_att + 2 * att_dim:col_att + 3 * att_dim]
        outs["conv_p"].append(conv_tail[mp // CHUNK - 1, 6:8][None])
        outs["conv_s"].append(conv_tail[mp // CHUNK:, 6:8])
        outs["k_p"].append(kn[mp - keep:mp].reshape(nb, keep, ATT_HEADS, ATT_HEAD_DIM))
        outs["v_p"].append(v_new[mp - keep:mp].reshape(nb, keep, ATT_HEADS, ATT_HEAD_DIM))
        outs["k_s"].append(kn[mp:].reshape(nd, ds, ATT_HEADS, ATT_HEAD_DIM))
        outs["v_s"].append(v_new[mp:].reshape(nd, ds, ATT_HEADS, ATT_HEAD_DIM))
        outs["gla_p"].append(s_prompt)
        outs["gla_s"].append(s_sample)

    st = {name: jnp.stack(v) for name, v in outs.items()}
    return (xp.reshape(nb, seq, d), xs.reshape(nd, ds, d),
            st["conv_p"], st["k_p"], st["v_p"], st["gla_p"],
            st["conv_s"], st["k_s"], st["v_s"], st["gla_s"])
```

```python
import functools

import jax
import jax.numpy as jnp
from jax import lax
from jax.experimental import pallas as pl
from jax.experimental.pallas import tpu as pltpu

F32 = jnp.float32
BF16 = jnp.bfloat16
HIGHEST = lax.Precision.HIGHEST

CHUNK = 64
BAND_CHUNKS = 8
WINDOW = BAND_CHUNKS * CHUNK
SPAN = WINDOW + CHUNK
REL_CLIP = 128
ATT_HEADS = 8
ATT_HEAD_DIM = 128
GLA_HEADS = 4
GLA_DK = 256
GLA_DV = 512
GLA_RANK = 16
GLA_TAU = 16.0
GLA_CHUNK = 32
LA_PAD = 128
PROMPT_ROW_GROUPS = 2
NEG_INF = -1e30
EPS = 1e-6
VMEM_LIMIT = 56 * 1024 * 1024

NT_DIMS = (((1,), (1,)), ((), ()))
TN_DIMS = (((0,), (0,)), ((), ()))


def _params(sem, vmem=None):
    return pltpu.CompilerParams(dimension_semantics=sem, vmem_limit_bytes=vmem)


def _rms(x, g):
    return x * lax.rsqrt(jnp.mean(x * x, axis=-1, keepdims=True) + EPS) * g


def _split_row_specs(tr, tn, n_prompt_tiles, row_axis, col_of):
    def prompt_map(*ids):
        return (jnp.minimum(ids[row_axis], n_prompt_tiles - 1), col_of(*ids))

    def sample_map(*ids):
        return (jnp.maximum(ids[row_axis] - n_prompt_tiles, 0), col_of(*ids))

    return pl.BlockSpec((tr, tn), prompt_map), pl.BlockSpec((tr, tn), sample_map)


def _pick_rows(i, n_prompt_tiles, p_ref, s_ref):
    return jnp.where(i < n_prompt_tiles, p_ref[...], s_ref[...])


def _norm_body(n_prompt_tiles, xp_ref, xs_ref, g_ref, h_ref):
    x = _pick_rows(pl.program_id(0), n_prompt_tiles, xp_ref, xs_ref)
    h_ref[...] = _rms(x, g_ref[...]).astype(h_ref.dtype)


def _norm_la_body(n_prompt_tiles, xp_ref, xs_ref, g_ref, wla_ref, h_ref, la_ref, wlab_ref):
    rank = wla_ref.shape[0]

    @pl.when(pl.program_id(0) == 0)
    def _():
        wlab_ref[...] = jnp.zeros_like(wlab_ref)
        wlab_ref[0:rank, :] = wla_ref[...].astype(wlab_ref.dtype)

    x = _pick_rows(pl.program_id(0), n_prompt_tiles, xp_ref, xs_ref)
    h = _rms(x, g_ref[...]).astype(h_ref.dtype)
    h_ref[...] = h
    la_ref[...] = lax.dot_general(h, wlab_ref[...], NT_DIMS, preferred_element_type=F32)


def rmsnorm_bf16(xp, xs, g, tr=512):
    (mp, d), ms = xp.shape, xs.shape[0]
    assert mp % tr == 0 and ms % tr == 0
    p_spec, s_spec = _split_row_specs(tr, d, mp // tr, 0, lambda i: 0)
    return pl.pallas_call(
        functools.partial(_norm_body, mp // tr), grid=((mp + ms) // tr,),
        in_specs=[p_spec, s_spec, pl.BlockSpec((1, d), lambda i: (0, 0))],
        out_specs=pl.BlockSpec((tr, d), lambda i: (i, 0)),
        out_shape=jax.ShapeDtypeStruct((mp + ms, d), BF16),
        compiler_params=_params(("parallel",), VMEM_LIMIT), name="rmsnorm",
    )(xp, xs, g.reshape(1, d))


def rmsnorm_la(xp, xs, g, w_t, layer, la_row, rank, tr=512):
    (mp, d), ms = xp.shape, xs.shape[0]
    assert mp % tr == 0 and ms % tr == 0
    m = mp + ms
    p_spec, s_spec = _split_row_specs(tr, d, mp // tr, 0, lambda i: 0)
    return pl.pallas_call(
        functools.partial(_norm_la_body, mp // tr), grid=(m // tr,),
        in_specs=[p_spec, s_spec, pl.BlockSpec((1, d), lambda i: (0, 0)),
                  pl.BlockSpec((None, rank, d), lambda i: (layer, la_row // rank, 0))],
        out_specs=[pl.BlockSpec((tr, d), lambda i: (i, 0)), pl.BlockSpec((tr, LA_PAD), lambda i: (i, 0))],
        out_shape=[jax.ShapeDtypeStruct((m, d), BF16), jax.ShapeDtypeStruct((m, LA_PAD), F32)],
        scratch_shapes=[pltpu.VMEM((LA_PAD, d), BF16)],
        compiler_params=_params(("arbitrary",), VMEM_LIMIT), name="rmsnorm_la",
    )(xp, xs, g.reshape(1, d), w_t)


CAST_ROWS = 512


def _in_proj_body(layer, n_aligned, skip, a_ref, w_hbm, o_ref, wf_ref, wb_ref, sem):
    j = pl.program_id(0)
    i = pl.program_id(1)
    tn = wb_ref.shape[0]

    def weight_copy(jj):
        row0 = pl.multiple_of(jj * tn + jnp.where(jj >= n_aligned, skip, 0), 16)
        return pltpu.make_async_copy(w_hbm.at[layer, pl.ds(row0, tn), :], wf_ref, sem)

    @pl.when(jnp.logical_and(j == 0, i == 0))
    def _():
        weight_copy(0).start()

    @pl.when(i == 0)
    def _():
        weight_copy(j).wait()
        for r in range(0, tn, 128):
            wb_ref[r:r + 128, :] = wf_ref[r:r + 128, :].astype(wb_ref.dtype)

        @pl.when(j + 1 < pl.num_programs(0))
        def _():
            weight_copy(j + 1).start()

    o_ref[...] = lax.dot_general(a_ref[...], wb_ref[...], NT_DIMS, preferred_element_type=F32)


def in_proj(a, w_t, layer, skip_row, skip, tm=1024, tn=1024):
    m, k = a.shape
    n = w_t.shape[1] - skip
    assert skip_row % tn == 0 and n % tn == 0 and skip % 16 == 0
    return pl.pallas_call(
        functools.partial(_in_proj_body, layer, skip_row // tn, skip), grid=(n // tn, m // tm),
        in_specs=[pl.BlockSpec((tm, k), lambda j, i: (i, 0)),
                  pl.BlockSpec(memory_space=pl.ANY)],
        out_specs=pl.BlockSpec((tm, tn), lambda j, i: (i, j)),
        out_shape=jax.ShapeDtypeStruct((m, n), F32),
        scratch_shapes=[pltpu.VMEM((tn, k), F32), pltpu.VMEM((tn, k), BF16), pltpu.SemaphoreType.DMA(())],
        compiler_params=_params(("arbitrary", "arbitrary"), VMEM_LIMIT), name="in_proj",
    )(a, w_t)


def _mm_res_body(n_prompt_tiles, a_ref, b_ref, rp_ref, rs_ref, op_ref, os_ref):
    i = pl.program_id(1)
    acc = jnp.dot(a_ref[...], b_ref[...], preferred_element_type=F32)

    @pl.when(i < n_prompt_tiles)
    def _():
        op_ref[...] = rp_ref[...] + acc

    @pl.when(i >= n_prompt_tiles)
    def _():
        os_ref[...] = rs_ref[...] + acc


def matmul_residual(a, b, layer, res_p, res_s, tm=1024, tn=512):
    m, k = a.shape
    n = b.shape[2]
    mp, ms = res_p.shape[0], res_s.shape[0]
    assert mp % tm == 0 and ms % tm == 0 and mp + ms == m
    p_spec, s_spec = _split_row_specs(tm, tn, mp // tm, 1, lambda j, i: j)
    return pl.pallas_call(
        functools.partial(_mm_res_body, mp // tm), grid=(n // tn, m // tm),
        in_specs=[pl.BlockSpec((tm, k), lambda j, i: (i, 0)),
                  pl.BlockSpec((None, k, tn), lambda j, i: (layer, 0, j)),
                  p_spec, s_spec],
        out_specs=[p_spec, s_spec],
        out_shape=[jax.ShapeDtypeStruct((mp, n), F32), jax.ShapeDtypeStruct((ms, n), F32)],
        compiler_params=_params(("parallel", "arbitrary"), VMEM_LIMIT), name="matmul_residual",
    )(a, b, res_p, res_s)


def _swiglu_body(layer, nb, row_groups, a_ref, w_hbm, o_ref, wf_ref, wb_ref, sem):
    j = pl.program_id(0)
    i = pl.program_id(1)
    k, tn = wf_ref.shape[1], wf_ref.shape[2]

    def weight_copies(jj):
        return [pltpu.make_async_copy(w_hbm.at[layer, :, pl.ds(pl.multiple_of((jj + half * nb) * tn, tn), tn)],
                                      wf_ref.at[half], sem.at[half]) for half in range(2)]

    @pl.when(jnp.logical_and(j == 0, i == 0))
    def _():
        for c in weight_copies(0):
            c.start()

    @pl.when(i == 0)
    def _():
        for c in weight_copies(j):
            c.wait()

        def step(r, carry):
            rs = pl.ds(pl.multiple_of(r * CAST_ROWS, CAST_ROWS), CAST_ROWS)
            wb_ref[rs, 0:tn] = wf_ref[0, rs, :].astype(wb_ref.dtype)
            wb_ref[rs, tn:2 * tn] = wf_ref[1, rs, :].astype(wb_ref.dtype)
            return carry
        lax.fori_loop(0, k // CAST_ROWS, step, 0)

        @pl.when(j + 1 < pl.num_programs(0))
        def _():
            for c in weight_copies(j + 1):
                c.start()

    rg = a_ref.shape[0] // row_groups
    for r in range(0, a_ref.shape[0], rg):
        u = jnp.dot(a_ref[r:r + rg, :], wb_ref[...], preferred_element_type=F32)
        ug = u[:, 0:tn]
        o_ref[r:r + rg, :] = (ug * jax.nn.sigmoid(ug) * u[:, tn:2 * tn]).astype(o_ref.dtype)


def swiglu_up(a, w_gu, layer, tm=2304, tn=256, row_groups=2):
    m, k = a.shape
    dff = w_gu.shape[2] // 2
    nb = dff // tn
    assert m % tm == 0 and dff % tn == 0 and k % CAST_ROWS == 0 and tm % (8 * row_groups) == 0
    return pl.pallas_call(
        functools.partial(_swiglu_body, layer, nb, row_groups), grid=(nb, m // tm),
        in_specs=[pl.BlockSpec((tm, k), lambda j, i: (i, 0)),
                  pl.BlockSpec(memory_space=pl.ANY)],
        out_specs=pl.BlockSpec((tm, tn), lambda j, i: (i, j)),
        out_shape=jax.ShapeDtypeStruct((m, dff), BF16),
        scratch_shapes=[pltpu.VMEM((2, k, tn), F32), pltpu.VMEM((k, 2 * tn), BF16),
                        pltpu.SemaphoreType.DMA((2,))],
        compiler_params=_params(("arbitrary", "arbitrary"), VMEM_LIMIT), name="swiglu_up",
    )(a, w_gu)


def _conv_body(n_prompt_tiles, group, cb_ref, cc_ref, cx_ref, pc_ref, px_ref, cache_ref, w_ref,
               y_ref, tail_ref):
    t = pl.program_id(0)
    u = cc_ref[...] * cx_ref[...]
    rows = u.shape[0]
    row = lax.broadcasted_iota(jnp.int32, u.shape, 0)
    u1 = pltpu.roll(u, 1, 0)
    u2 = pltpu.roll(u, 2, 0)
    w = w_ref[...]

    def finish(um1, um2):
        y = um2 * w[0:1] + um1 * w[1:2] + u * w[2:3]
        y_ref[...] = (cb_ref[...] * y).astype(y_ref.dtype)

    @pl.when(t < n_prompt_tiles)
    def _():
        prev = jnp.where(t > 0, pc_ref[...] * px_ref[...], 0.0)
        finish(jnp.where(row == 0, prev[7:8], u1),
               jnp.where(row == 0, prev[6:7], jnp.where(row == 1, prev[7:8], u2)))

    @pl.when(t >= n_prompt_tiles)
    def _():
        um1, um2 = u1, u2
        for g in range(rows // group):
            c0 = cache_ref[g, 0:1]
            c1 = cache_ref[g, 1:2]
            um1 = jnp.where(row == g * group, c1, um1)
            um2 = jnp.where(row == g * group, c0, jnp.where(row == g * group + 1, c1, um2))
        finish(um1, um2)

    for g in range(rows // group):
        tail_ref[g] = u[(g + 1) * group - 8:(g + 1) * group]


def gated_conv(z, cache, w, n_prompt_rows, group=CHUNK, tr=512):
    m = z.shape[0]
    c = w.shape[1]
    gpt = tr // group
    npt = n_prompt_rows // tr
    assert n_prompt_rows % tr == 0 and m % tr == 0 and tr % group == 0
    row_spec = lambda col: pl.BlockSpec((tr, c), lambda t: (t, col))
    prev_spec = lambda col: pl.BlockSpec((8, c), lambda t: (jnp.maximum(t * (tr // 8) - 1, 0), col))
    return pl.pallas_call(
        functools.partial(_conv_body, npt, group), grid=(m // tr,),
        in_specs=[row_spec(0), row_spec(1), row_spec(2), prev_spec(1), prev_spec(2),
                  pl.BlockSpec((gpt, 2, c), lambda t: (jnp.maximum(t - npt, 0), 0, 0)),
                  pl.BlockSpec((3, c), lambda t: (0, 0))],
        out_specs=[pl.BlockSpec((tr, c), lambda t: (t, 0)),
                   pl.BlockSpec((gpt, 8, c), lambda t: (t, 0, 0))],
        out_shape=[jax.ShapeDtypeStruct((m, c), BF16), jax.ShapeDtypeStruct((m // group, 8, c), F32)],
        compiler_params=_params(("parallel",), VMEM_LIMIT), name="gated_conv",
    )(z, z, z, z, z, cache, w)


def _qk_body(heads, aq_ref, ak_ref, av_ref, gq_ref, gk_ref, qn_ref, kn_ref, kpad_ref, vpad_ref):
    s = pl.program_id(0)

    @pl.when(s == 0)
    def _():
        kpad_ref[...] = jnp.zeros_like(kpad_ref)
        vpad_ref[...] = jnp.zeros_like(vpad_ref)

    @pl.when(s > 0)
    def _():
        dh = gq_ref.shape[1]
        gq = gq_ref[...]
        gk = gk_ref[...]
        for h in range(heads):
            sl = slice(h * dh, (h + 1) * dh)
            qn_ref[:, sl] = _rms(aq_ref[:, sl], gq).astype(qn_ref.dtype)
            kn = _rms(ak_ref[:, sl], gk)
            kn_ref[:, sl] = kn
            kpad_ref[:, sl] = kn.astype(kpad_ref.dtype)
        vpad_ref[...] = av_ref[...].astype(vpad_ref.dtype)


def qk_norm(z, g_q, g_k, col0, heads, tr=WINDOW):
    m = z.shape[0]
    dh = g_q.shape[0]
    c = heads * dh
    assert tr == WINDOW and m % tr == 0
    tile = lambda s: jnp.maximum(s - 1, 0)
    spec = lambda col: pl.BlockSpec((tr, c), lambda s: (tile(s), col))
    g_spec = pl.BlockSpec((1, dh), lambda s: (0, 0))
    o_spec = pl.BlockSpec((tr, c), lambda s: (tile(s), 0))
    pad_spec = pl.BlockSpec((tr, c), lambda s: (s, 0))
    return pl.pallas_call(
        functools.partial(_qk_body, heads), grid=(1 + m // tr,),
        in_specs=[spec(col0), spec(col0 + 1), spec(col0 + 2), g_spec, g_spec],
        out_specs=[o_spec, o_spec, pad_spec, pad_spec],
        out_shape=[jax.ShapeDtypeStruct((m, c), BF16), jax.ShapeDtypeStruct((m, c), F32),
                   jax.ShapeDtypeStruct((WINDOW + m, c), BF16), jax.ShapeDtypeStruct((WINDOW + m, c), BF16)],
        compiler_params=_params(("arbitrary",), VMEM_LIMIT), name="qk_norm",
    )(z, z, z, g_q.reshape(1, dh), g_k.reshape(1, dh))


def _toeplitz_bias(vec, n_queries, n_keys):
    lanes = vec.shape[-1]
    return pltpu.roll(jnp.broadcast_to(vec, (n_queries, lanes)), 0, 1, stride=1, stride_axis=0)[:, :n_keys]


def _softmax_pv(s, v):
    p = jnp.exp(s - jnp.max(s, axis=-1, keepdims=True))
    p = p * (1.0 / jnp.sum(p, axis=-1, keepdims=True))
    return jnp.dot(p.astype(v.dtype), v, preferred_element_type=F32)


def _prompt_attn_body(chunks_per_tile, scale, q_ref, k_ref, v_ref, tbias_ref, o_ref, tile_bias_ref):
    t = pl.program_id(1)
    tr = chunks_per_tile * CHUNK
    tk = tr + WINDOW

    @pl.when(t == 0)
    def _():
        row = lax.broadcasted_iota(jnp.int32, (tr, tk), 0)
        col = lax.broadcasted_iota(jnp.int32, (tr, tk), 1)
        first = jnp.left_shift(jnp.right_shift(row, CHUNK.bit_length() - 1), CHUNK.bit_length() - 1)
        in_band = jnp.logical_and(col >= first, col < first + SPAN)
        tile_bias_ref[...] = jnp.where(in_band, _toeplitz_bias(tbias_ref[0], tr, tk), NEG_INF)

    start = pl.multiple_of(t * tr, tr)
    k = k_ref[pl.ds(start, tk), :]
    v = v_ref[pl.ds(start, tk), :]
    rg = tr // PROMPT_ROW_GROUPS
    col = lax.broadcasted_iota(jnp.int32, (rg, tk), 1)
    for r in range(0, tr, rg):
        s = lax.dot_general(q_ref[r:r + rg, :], k, NT_DIMS, preferred_element_type=F32)
        s = s * scale + tile_bias_ref[r:r + rg, :]
        s = jnp.where(col >= WINDOW - t * tr, s, NEG_INF)
        o_ref[r:r + rg, :] = _softmax_pv(s, v).astype(o_ref.dtype)


def prompt_attention(qn, keys, vals, rel_bias, n_prompt_rows, chunks_per_tile=8):
    c = qn.shape[1]
    heads = rel_bias.shape[0]
    dh = c // heads
    tr = CHUNK * chunks_per_tile
    tk = tr + WINDOW
    kr = keys.shape[0]
    assert n_prompt_rows % tr == 0 and kr >= n_prompt_rows + WINDOW
    return pl.pallas_call(
        functools.partial(_prompt_attn_body, chunks_per_tile, dh ** -0.5),
        grid=(heads, n_prompt_rows // tr),
        in_specs=[pl.BlockSpec((tr, dh), lambda h, t: (t, h)),
                  pl.BlockSpec((kr, dh), lambda h, t: (0, h)),
                  pl.BlockSpec((kr, dh), lambda h, t: (0, h)),
                  pl.BlockSpec((1, 1, tk + tr), lambda h, t: (h, 0, 0))],
        out_specs=pl.BlockSpec((tr, dh), lambda h, t: (t, h)),
        out_shape=jax.ShapeDtypeStruct((n_prompt_rows, c), BF16),
        scratch_shapes=[pltpu.VMEM((tr, tk), F32)],
        compiler_params=_params(("parallel", "arbitrary"), VMEM_LIMIT), name="prompt_attention",
    )(qn, keys, vals, rel_bias_rows(rel_bias, tr, tk))


def _sample_attn_body(heads, scale, q_ref, kn_ref, vn_ref, ck_ref, cv_ref, bias_ref, o_ref):
    dh = q_ref.shape[1] // heads
    for h in range(heads):
        sl = slice(h * dh, (h + 1) * dh)
        k = jnp.concatenate([ck_ref[pl.ds(h, WINDOW, stride=heads), :].astype(kn_ref.dtype), kn_ref[:, sl]], axis=0)
        v = jnp.concatenate([cv_ref[pl.ds(h, WINDOW, stride=heads), :].astype(vn_ref.dtype), vn_ref[:, sl]], axis=0)
        s = lax.dot_general(q_ref[:, sl], k, NT_DIMS, preferred_element_type=F32)
        s = s * scale + _toeplitz_bias(bias_ref[h], CHUNK, SPAN)
        o_ref[:, sl] = _softmax_pv(s, v).astype(o_ref.dtype)


def sample_attention(qn, keys, vals, cache_k, cache_v, layer, rel_bias, n_prompt_rows):
    m, c = qn.shape
    depth, nseq, win, heads, dh = cache_k.shape
    assert win == WINDOW and heads * dh == c and m - n_prompt_rows == nseq * CHUNK
    q0 = n_prompt_rows // CHUNK
    k0 = (n_prompt_rows + WINDOW) // CHUNK
    cache_spec = pl.BlockSpec((None, None, WINDOW * heads, dh), lambda n: (layer, n, 0, 0))
    return pl.pallas_call(
        functools.partial(_sample_attn_body, heads, dh ** -0.5), grid=(nseq,),
        in_specs=[pl.BlockSpec((CHUNK, c), lambda n: (q0 + n, 0)),
                  pl.BlockSpec((CHUNK, c), lambda n: (k0 + n, 0)),
                  pl.BlockSpec((CHUNK, c), lambda n: (k0 + n, 0)),
                  cache_spec, cache_spec,
                  pl.BlockSpec((heads, 1, SPAN + CHUNK), lambda n: (0, 0, 0))],
        out_specs=pl.BlockSpec((CHUNK, c), lambda n: (n, 0)),
        out_shape=jax.ShapeDtypeStruct((nseq * CHUNK, c), BF16),
        compiler_params=_params(("parallel",), VMEM_LIMIT), name="sample_attention",
    )(qn, keys, vals, cache_k.reshape(depth, nseq, WINDOW * heads, dh),
      cache_v.reshape(depth, nseq, WINDOW * heads, dh), rel_bias_rows(rel_bias, CHUNK, SPAN))


def rel_bias_rows(rel_bias, n_queries, n_keys):
    lanes = n_keys + n_queries
    i = jnp.arange(lanes)
    dist = jnp.where(i < n_keys, -i, lanes - i)
    rel = jnp.clip(dist + WINDOW, -REL_CLIP, REL_CLIP) + REL_CLIP
    return rel_bias.astype(F32)[:, rel].reshape(rel_bias.shape[0], 1, lanes)


def _gla_body(n_prompt_tiles, seg_rows, lq_ref, lk_ref, lv_ref, lr_ref, la_ref, wa2_ref, ba_ref, g_ref,
              s0_ref, y_ref, soutp_ref, souts_ref, state_ref, o_ref):
    t = pl.program_id(0)
    rows = lq_ref.shape[0]
    heads, dv, dk = state_ref.shape
    cb = GLA_CHUNK
    nc = rows // cb
    chunks_per_seg = seg_rows // cb
    nseg = rows // seg_rows
    is_sample = t >= n_prompt_tiles
    shift = cb.bit_length() - 1

    r_i = lax.broadcasted_iota(jnp.int32, (rows, rows), 0)
    c_i = lax.broadcasted_iota(jnp.int32, (rows, rows), 1)
    same_chunk = jnp.right_shift(r_i, shift) == jnp.right_shift(c_i, shift)
    chunk_tri = jnp.logical_and(same_chunk, r_i >= c_i).astype(F32)
    causal = (lax.broadcasted_iota(jnp.int32, (cb, cb), 0)
              >= lax.broadcasted_iota(jnp.int32, (cb, cb), 1))[None]

    kd = heads * dk
    x = jnp.dot(la_ref[...], wa2_ref[...], precision=HIGHEST, preferred_element_type=F32) + ba_ref[...]
    log_a = (jnp.minimum(x, 0.0) - jnp.log1p(jnp.exp(-jnp.abs(x)))) * (1.0 / GLA_TAU)
    b = jnp.dot(chunk_tri, log_a, precision=HIGHEST, preferred_element_type=F32).reshape(nc, cb, kd)
    b_mid = b[:, cb // 2 - 1:cb // 2]
    b_last = b[:, cb - 1:cb]
    q = (lq_ref[...] * (dk ** -0.5)).reshape(nc, cb, kd)
    k = lk_ref[...].reshape(nc, cb, kd)
    q_dec_all = (q * jnp.exp(b)).astype(BF16)
    q_mid_all = (q * jnp.exp(b - b_mid)).astype(BF16)
    k_mid_all = (k * jnp.exp(b_mid - b)).astype(BF16)
    k_rem_all = (k * jnp.exp(b_last - b)).astype(BF16)
    decay_all = jnp.exp(b_last)
    pre = []
    for h in range(heads):
        ks = slice(h * dk, (h + 1) * dk)
        v = lv_ref[:, h * dv:(h + 1) * dv].astype(BF16).reshape(nc, cb, dv)
        att = jnp.einsum("clk,cmk->clm", q_mid_all[:, :, ks], k_mid_all[:, :, ks],
                         preferred_element_type=F32)
        att = jnp.where(causal, att, 0.0).astype(BF16)
        o_intra = jnp.einsum("clm,cmv->clv", att, v, preferred_element_type=F32)
        pre.append((q_dec_all[:, :, ks], k_rem_all[:, :, ks], v, o_intra, decay_all[:, :, ks]))

    for s in range(nseg):
        @pl.when(is_sample)
        def _():
            for h in range(heads):
                state_ref[h] = s0_ref[s, h].T

        if s == 0:
            @pl.when(t == 0)
            def _():
                state_ref[...] = jnp.zeros_like(state_ref)

        for c in range(s * chunks_per_seg, (s + 1) * chunks_per_seg):
            for h in range(heads):
                q_dec, k_rem, v, o_intra, decay = pre[h]
                state = state_ref[h]
                o = o_intra[c] + lax.dot_general(q_dec[c], state.astype(BF16), NT_DIMS,
                                                 preferred_element_type=F32)
                kv = lax.dot_general(v[c], k_rem[c], TN_DIMS, preferred_element_type=F32)
                state_ref[h] = decay[c] * state + kv
                o_ref[c * cb:(c + 1) * cb, h * dv:(h + 1) * dv] = o

        @pl.when(is_sample)
        def _():
            for h in range(heads):
                souts_ref[s, h] = state_ref[h].T

        if s == nseg - 1:
            @pl.when(t == n_prompt_tiles - 1)
            def _():
                for h in range(heads):
                    soutp_ref[0, h] = state_ref[h].T

    g = g_ref[...]
    for h in range(heads):
        vs = slice(h * dv, (h + 1) * dv)
        r = lr_ref[:, vs]
        y_ref[:, vs] = (_rms(o_ref[:, vs], g) * (r * jax.nn.sigmoid(r))).astype(y_ref.dtype)


def gla(z, la, w_a2, b_a, g_gla, s0, layer, n_prompt_rows, seg_rows, col_q, col_k, col_v, col_r, tr=128):
    m = z.shape[0]
    _, nseq, heads, dk, dv = s0.shape
    kd, vd = heads * dk, heads * dv
    npt = n_prompt_rows // tr
    spt = tr // seg_rows
    assert n_prompt_rows % tr == 0 and tr % seg_rows == 0 and seg_rows % GLA_CHUNK == 0
    sample_tile = lambda t: (jnp.maximum(t - npt, 0), 0, 0, 0)
    wa2p = jnp.zeros((LA_PAD, kd), F32).at[:w_a2.shape[0]].set(w_a2)
    return pl.pallas_call(
        functools.partial(_gla_body, npt, seg_rows), grid=(m // tr,),
        in_specs=[pl.BlockSpec((tr, kd), lambda t: (t, col_q // kd)),
                  pl.BlockSpec((tr, kd), lambda t: (t, col_k // kd)),
                  pl.BlockSpec((tr, vd), lambda t: (t, col_v // vd)),
                  pl.BlockSpec((tr, vd), lambda t: (t, col_r // vd)),
                  pl.BlockSpec((tr, LA_PAD), lambda t: (t, 0)),
                  pl.BlockSpec((LA_PAD, kd), lambda t: (0, 0)),
                  pl.BlockSpec((1, kd), lambda t: (0, 0)),
                  pl.BlockSpec((1, dv), lambda t: (0, 0)),
                  pl.BlockSpec((None, spt, heads, dk, dv), lambda t: (layer,) + sample_tile(t))],
        out_specs=[pl.BlockSpec((tr, vd), lambda t: (t, 0)),
                   pl.BlockSpec((1, heads, dk, dv), lambda t: (0, 0, 0, 0)),
                   pl.BlockSpec((spt, heads, dk, dv), sample_tile)],
        out_shape=[jax.ShapeDtypeStruct((m, vd), BF16),
                   jax.ShapeDtypeStruct((1, heads, dk, dv), F32),
                   jax.ShapeDtypeStruct((nseq, heads, dk, dv), F32)],
        scratch_shapes=[pltpu.VMEM((heads, dv, dk), F32), pltpu.VMEM((tr, vd), F32)],
        compiler_params=_params(("arbitrary",), VMEM_LIMIT), name="gla",
    )(z, z, z, z, la, wa2p, b_a.reshape(1, -1), g_gla.reshape(1, -1), s0)


def _branch_body(n_prompt_tiles, ya_ref, ybp_ref, ybs_ref, yc_ref, wa_ref, wb_ref, wc_ref,
                 ga_ref, gb_ref, gc_ref, o_ref):
    yb = _pick_rows(pl.program_id(0), n_prompt_tiles, ybp_ref, ybs_ref)
    m = jax.nn.sigmoid(ga_ref[...]) * jnp.dot(ya_ref[...], wa_ref[...], preferred_element_type=F32)
    m = m + jax.nn.sigmoid(gb_ref[...]) * jnp.dot(yb, wb_ref[...], preferred_element_type=F32)
    m = m + jax.nn.sigmoid(gc_ref[...]) * jnp.dot(yc_ref[...], wc_ref[...], preferred_element_type=F32)
    o_ref[...] = m.astype(o_ref.dtype)


def branch_merge(ya, yb_p, yb_s, yc, w_branch, layer, z, gate_col, tm=1024, tn=512):
    m = ya.shape[0]
    ca, cb, cc = ya.shape[1], yb_p.shape[1], yc.shape[1]
    mp = yb_p.shape[0]
    d = w_branch.shape[2]
    assert ca == cb and cc % ca == 0 and (ca + cb) % cc == 0
    assert mp % tm == 0 and yb_s.shape[0] % tm == 0 and mp + yb_s.shape[0] == m
    gb = gate_col // tn
    nd = d // tn
    ybp_spec, ybs_spec = _split_row_specs(tm, cb, mp // tm, 0, lambda i, j: 0)
    return pl.pallas_call(
        functools.partial(_branch_body, mp // tm), grid=(m // tm, nd),
        in_specs=[pl.BlockSpec((tm, ca), lambda i, j: (i, 0)),
                  ybp_spec, ybs_spec,
                  pl.BlockSpec((tm, cc), lambda i, j: (i, 0)),
                  pl.BlockSpec((None, ca, tn), lambda i, j: (layer, 0, j)),
                  pl.BlockSpec((None, cb, tn), lambda i, j: (layer, 1, j)),
                  pl.BlockSpec((None, cc, tn), lambda i, j: (layer, (ca + cb) // cc, j)),
                  pl.BlockSpec((tm, tn), lambda i, j: (i, gb + j)),
                  pl.BlockSpec((tm, tn), lambda i, j: (i, gb + nd + j)),
                  pl.BlockSpec((tm, tn), lambda i, j: (i, gb + 2 * nd + j))],
        out_specs=pl.BlockSpec((tm, tn), lambda i, j: (i, j)),
        out_shape=jax.ShapeDtypeStruct((m, d), BF16),
        compiler_params=_params(("parallel", "parallel"), VMEM_LIMIT), name="branch_merge",
    )(ya, yb_p, yb_s, yc, w_branch, w_branch, w_branch, z, z, z)


def kernel(x_prompt, x_sample, cache_conv, cache_k, cache_v, state_gla, g_mix, w_in, conv_w, g_q, g_k,
           rel_bias, w_a2, b_a, g_gla, w_branch, w_out, g_ffn, w_gu, w_down):
    nb, seq, d = x_prompt.shape
    nd, ds, _ = x_sample.shape
    depth = g_mix.shape[0]
    assert nb == 1 and ds == CHUNK
    mp = nb * seq
    ms = nd * ds
    conv_dim = conv_w.shape[2]
    att_dim = ATT_HEADS * ATT_HEAD_DIM
    kdim = GLA_HEADS * GLA_DK
    vdim = GLA_HEADS * GLA_DV
    col_att = 3 * conv_dim
    col_lq = col_att + 3 * att_dim
    col_lk = col_lq + kdim
    col_lv = col_lk + kdim
    col_lr = col_lv + vdim
    col_la = col_lr + vdim
    col_gate = col_la
    keep = min(WINDOW, seq)

    xp = x_prompt.reshape(mp, d)
    xs = x_sample.reshape(ms, d)
    w_branch_b = w_branch.astype(BF16)
    w_out_b = w_out.astype(BF16)
    w_down_b = w_down.astype(BF16)
    w_in_t = jnp.swapaxes(w_in, 1, 2)
    outs = {name: [] for name in ("conv_p", "k_p", "v_p", "gla_p", "conv_s", "k_s", "v_s", "gla_s")}
    for l in range(depth):
        h, la = rmsnorm_la(xp, xs, g_mix[l], w_in_t, l, col_la, GLA_RANK)
        z = in_proj(h, w_in_t, l, col_la, GLA_RANK)

        y_a, conv_tail = gated_conv(z, cache_conv[l], conv_w[l], mp)

        qn, kn, keys, vals = qk_norm(z, g_q[l], g_k[l], col_att // att_dim, ATT_HEADS)
        yb_p = prompt_attention(qn, keys, vals, rel_bias[l], mp)
        yb_s = sample_attention(qn, keys, vals, cache_k, cache_v, l, rel_bias[l], mp)

        y_c, s_prompt, s_sample = gla(z, la, w_a2[l], b_a[l], g_gla[l], state_gla, l, mp, ds,
                                      col_lq, col_lk, col_lv, col_lr)

        mrg = branch_merge(y_a, yb_p, yb_s, y_c, w_branch_b, l, z, col_gate)
        xp, xs = matmul_residual(mrg, w_out_b, l, xp, xs)

        hf = rmsnorm_bf16(xp, xs, g_ffn[l])
        hid = swiglu_up(hf, w_gu, l)
        xp, xs = matmul_residual(hid, w_down_b, l, xp, xs, tm=512, tn=512)

        v_new = z[:, col_att + 2 * att_dim:col_att + 3 * att_dim]
        outs["conv_p"].append(conv_tail[mp // CHUNK - 1, 6:8][None])
        outs["conv_s"].append(conv_tail[mp // CHUNK:, 6:8])
        outs["k_p"].append(kn[mp - keep:mp].reshape(nb, keep, ATT_HEADS, ATT_HEAD_DIM))
        outs["v_p"].append(v_new[mp - keep:mp].reshape(nb, keep, ATT_HEADS, ATT_HEAD_DIM))
        outs["k_s"].append(kn[mp:].reshape(nd, ds, ATT_HEADS, ATT_HEAD_DIM))
        outs["v_s"].append(v_new[mp:].reshape(nd, ds, ATT_HEADS, ATT_HEAD_DIM))
        outs["gla_p"].append(s_prompt)
        outs["gla_s"].append(s_sample)

    st = {name: jnp.stack(v) for name, v in outs.items()}
    return (xp.reshape(nb, seq, d), xs.reshape(nd, ds, d),
            st["conv_p"], st["k_p"], st["v_p"], st["gla_p"],
            st["conv_s"], st["k_s"], st["v_s"], st["gla_s"])
```

```python
import functools

import jax
import jax.numpy as jnp
from jax import lax
from jax.experimental import pallas as pl
from jax.experimental.pallas import tpu as pltpu

F32 = jnp.float32
BF16 = jnp.bfloat16
HIGHEST = lax.Precision.HIGHEST

CHUNK = 64
BAND_CHUNKS = 8
WINDOW = BAND_CHUNKS * CHUNK
SPAN = WINDOW + CHUNK
REL_CLIP = 128
ATT_HEADS = 8
ATT_HEAD_DIM = 128
GLA_HEADS = 4
GLA_DK = 256
GLA_DV = 512
GLA_RANK = 16
GLA_TAU = 16.0
GLA_CHUNK = 32
LA_PAD = 128
PROMPT_ROW_GROUPS = 2
NEG_INF = -1e30
EPS = 1e-6
VMEM_LIMIT = 56 * 1024 * 1024
IN_PROJ_VMEM_LIMIT = 60 * 1024 * 1024

NT_DIMS = (((1,), (1,)), ((), ()))
TN_DIMS = (((0,), (0,)), ((), ()))


def _params(sem, vmem=None):
    return pltpu.CompilerParams(dimension_semantics=sem, vmem_limit_bytes=vmem)


def _rms(x, g):
    return x * lax.rsqrt(jnp.mean(x * x, axis=-1, keepdims=True) + EPS) * g


def _split_row_specs(tr, tn, n_prompt_tiles, row_axis, col_of):
    def prompt_map(*ids):
        return (jnp.minimum(ids[row_axis], n_prompt_tiles - 1), col_of(*ids))

    def sample_map(*ids):
        return (jnp.maximum(ids[row_axis] - n_prompt_tiles, 0), col_of(*ids))

    return pl.BlockSpec((tr, tn), prompt_map), pl.BlockSpec((tr, tn), sample_map)


def _pick_rows(i, n_prompt_tiles, p_ref, s_ref):
    return jnp.where(i < n_prompt_tiles, p_ref[...], s_ref[...])


def _norm_body(n_prompt_tiles, xp_ref, xs_ref, g_ref, h_ref):
    x = _pick_rows(pl.program_id(0), n_prompt_tiles, xp_ref, xs_ref)
    h_ref[...] = _rms(x, g_ref[...]).astype(h_ref.dtype)


def _norm_la_body(n_prompt_tiles, xp_ref, xs_ref, g_ref, wla_ref, h_ref, la_ref, wlab_ref):
    rank = wla_ref.shape[0]

    @pl.when(pl.program_id(0) == 0)
    def _():
        wlab_ref[...] = jnp.zeros_like(wlab_ref)
        wlab_ref[0:rank, :] = wla_ref[...].astype(wlab_ref.dtype)

    x = _pick_rows(pl.program_id(0), n_prompt_tiles, xp_ref, xs_ref)
    h = _rms(x, g_ref[...]).astype(h_ref.dtype)
    h_ref[...] = h
    la_ref[...] = lax.dot_general(h, wlab_ref[...], NT_DIMS, preferred_element_type=F32)


def rmsnorm_bf16(xp, xs, g, tr=512):
    (mp, d), ms = xp.shape, xs.shape[0]
    assert mp % tr == 0 and ms % tr == 0
    p_spec, s_spec = _split_row_specs(tr, d, mp // tr, 0, lambda i: 0)
    return pl.pallas_call(
        functools.partial(_norm_body, mp // tr), grid=((mp + ms) // tr,),
        in_specs=[p_spec, s_spec, pl.BlockSpec((1, d), lambda i: (0, 0))],
        out_specs=pl.BlockSpec((tr, d), lambda i: (i, 0)),
        out_shape=jax.ShapeDtypeStruct((mp + ms, d), BF16),
        compiler_params=_params(("parallel",), VMEM_LIMIT), name="rmsnorm",
    )(xp, xs, g.reshape(1, d))


def rmsnorm_la(xp, xs, g, w_t, layer, la_row, rank, tr=512):
    (mp, d), ms = xp.shape, xs.shape[0]
    assert mp % tr == 0 and ms % tr == 0
    m = mp + ms
    p_spec, s_spec = _split_row_specs(tr, d, mp // tr, 0, lambda i: 0)
    return pl.pallas_call(
        functools.partial(_norm_la_body, mp // tr), grid=(m // tr,),
        in_specs=[p_spec, s_spec, pl.BlockSpec((1, d), lambda i: (0, 0)),
                  pl.BlockSpec((None, rank, d), lambda i: (layer, la_row // rank, 0))],
        out_specs=[pl.BlockSpec((tr, d), lambda i: (i, 0)), pl.BlockSpec((tr, LA_PAD), lambda i: (i, 0))],
        out_shape=[jax.ShapeDtypeStruct((m, d), BF16), jax.ShapeDtypeStruct((m, LA_PAD), F32)],
        scratch_shapes=[pltpu.VMEM((LA_PAD, d), BF16)],
        compiler_params=_params(("arbitrary",), VMEM_LIMIT), name="rmsnorm_la",
    )(xp, xs, g.reshape(1, d), w_t)


CAST_ROWS = 512


def _in_proj_body(layer, n_aligned, skip, a_ref, w_hbm, c1_ref, c2_ref, o_ref, c1b_ref, c2b_ref,
                  wf_ref, wb_ref, sem):
    j = pl.program_id(0)
    i = pl.program_id(1)
    tn = wb_ref.shape[0]

    c1b_ref[...] = c1_ref[...].astype(c1b_ref.dtype)
    c2b_ref[...] = c2_ref[...].astype(c2b_ref.dtype)

    def weight_copy(jj):
        row0 = pl.multiple_of(jj * tn + jnp.where(jj >= n_aligned, skip, 0), 16)
        return pltpu.make_async_copy(w_hbm.at[layer, pl.ds(row0, tn), :], wf_ref, sem)

    @pl.when(jnp.logical_and(j == 0, i == 0))
    def _():
        weight_copy(0).start()

    @pl.when(i == 0)
    def _():
        weight_copy(j).wait()
        for r in range(0, tn, 128):
            wb_ref[r:r + 128, :] = wf_ref[r:r + 128, :].astype(wb_ref.dtype)

        @pl.when(j + 1 < pl.num_programs(0))
        def _():
            weight_copy(j + 1).start()

    o_ref[...] = lax.dot_general(a_ref[...], wb_ref[...], NT_DIMS, preferred_element_type=F32)


def _ride_along_spec(shape, rows, layer, steps_per_outer):
    last = shape[1] // rows - 1
    return (pl.BlockSpec((None, rows, shape[2]),
                         lambda j, i: (layer, jnp.minimum(j * steps_per_outer + i, last), 0)),
            pl.BlockSpec((rows, shape[2]), lambda j, i: (jnp.minimum(j * steps_per_outer + i, last), 0)))


def in_proj(a, w_t, layer, skip_row, skip, cast1, cast2, tm=1024, tn=1024, cast_rows=32):
    m, k = a.shape
    n = w_t.shape[1] - skip
    steps = (n // tn) * (m // tm)
    assert skip_row % tn == 0 and n % tn == 0 and skip % 16 == 0
    for c in (cast1, cast2):
        assert c.shape[1] % cast_rows == 0 and c.shape[1] // cast_rows <= steps
    c1_in, c1_out = _ride_along_spec(cast1.shape, cast_rows, layer, m // tm)
    c2_in, c2_out = _ride_along_spec(cast2.shape, cast_rows, layer, m // tm)
    return pl.pallas_call(
        functools.partial(_in_proj_body, layer, skip_row // tn, skip), grid=(n // tn, m // tm),
        in_specs=[pl.BlockSpec((tm, k), lambda j, i: (i, 0)),
                  pl.BlockSpec(memory_space=pl.ANY), c1_in, c2_in],
        out_specs=[pl.BlockSpec((tm, tn), lambda j, i: (i, j)), c1_out, c2_out],
        out_shape=[jax.ShapeDtypeStruct((m, n), F32),
                   jax.ShapeDtypeStruct(cast1.shape[1:], BF16), jax.ShapeDtypeStruct(cast2.shape[1:], BF16)],
        scratch_shapes=[pltpu.VMEM((tn, k), F32), pltpu.VMEM((tn, k), BF16), pltpu.SemaphoreType.DMA(())],
        compiler_params=_params(("arbitrary", "arbitrary"), IN_PROJ_VMEM_LIMIT), name="in_proj",
    )(a, w_t, cast1, cast2)


def _mm_res_body(n_prompt_tiles, a_ref, b_ref, rp_ref, rs_ref, op_ref, os_ref):
    i = pl.program_id(1)
    acc = jnp.dot(a_ref[...], b_ref[...], preferred_element_type=F32)

    @pl.when(i < n_prompt_tiles)
    def _():
        op_ref[...] = rp_ref[...] + acc

    @pl.when(i >= n_prompt_tiles)
    def _():
        os_ref[...] = rs_ref[...] + acc


def matmul_residual(a, b, layer, res_p, res_s, tm=1024, tn=512):
    m, k = a.shape
    n = b.shape[2]
    mp, ms = res_p.shape[0], res_s.shape[0]
    assert mp % tm == 0 and ms % tm == 0 and mp + ms == m
    p_spec, s_spec = _split_row_specs(tm, tn, mp // tm, 1, lambda j, i: j)
    return pl.pallas_call(
        functools.partial(_mm_res_body, mp // tm), grid=(n // tn, m // tm),
        in_specs=[pl.BlockSpec((tm, k), lambda j, i: (i, 0)),
                  pl.BlockSpec((None, k, tn), lambda j, i: (layer, 0, j)),
                  p_spec, s_spec],
        out_specs=[p_spec, s_spec],
        out_shape=[jax.ShapeDtypeStruct((mp, n), F32), jax.ShapeDtypeStruct((ms, n), F32)],
        compiler_params=_params(("parallel", "arbitrary"), VMEM_LIMIT), name="matmul_residual",
    )(a, b, res_p, res_s)


def _swiglu_body(layer, nb, row_groups, a_ref, w_hbm, c_ref, o_ref, cb_ref, wf_ref, wb_ref, sem):
    j = pl.program_id(0)
    i = pl.program_id(1)
    k, tn = wf_ref.shape[1], wf_ref.shape[2]

    cb_ref[...] = c_ref[...].astype(cb_ref.dtype)

    def weight_copies(jj):
        return [pltpu.make_async_copy(w_hbm.at[layer, :, pl.ds(pl.multiple_of((jj + half * nb) * tn, tn), tn)],
                                      wf_ref.at[half], sem.at[half]) for half in range(2)]

    @pl.when(jnp.logical_and(j == 0, i == 0))
    def _():
        for c in weight_copies(0):
            c.start()

    @pl.when(i == 0)
    def _():
        for c in weight_copies(j):
            c.wait()

        def step(r, carry):
            rs = pl.ds(pl.multiple_of(r * CAST_ROWS, CAST_ROWS), CAST_ROWS)
            wb_ref[rs, 0:tn] = wf_ref[0, rs, :].astype(wb_ref.dtype)
            wb_ref[rs, tn:2 * tn] = wf_ref[1, rs, :].astype(wb_ref.dtype)
            return carry
        lax.fori_loop(0, k // CAST_ROWS, step, 0)

        @pl.when(j + 1 < pl.num_programs(0))
        def _():
            for c in weight_copies(j + 1):
                c.start()

    rg = a_ref.shape[0] // row_groups
    for r in range(0, a_ref.shape[0], rg):
        u = jnp.dot(a_ref[r:r + rg, :], wb_ref[...], preferred_element_type=F32)
        ug = u[:, 0:tn]
        o_ref[r:r + rg, :] = (ug * jax.nn.sigmoid(ug) * u[:, tn:2 * tn]).astype(o_ref.dtype)


def swiglu_up(a, w_gu, layer, cast, tm=2304, tn=256, row_groups=2, cast_rows=64):
    m, k = a.shape
    dff = w_gu.shape[2] // 2
    nb = dff // tn
    assert m % tm == 0 and dff % tn == 0 and k % CAST_ROWS == 0 and tm % (8 * row_groups) == 0
    assert cast.shape[1] % cast_rows == 0 and cast.shape[1] // cast_rows <= nb * (m // tm)
    c_in, c_out = _ride_along_spec(cast.shape, cast_rows, layer, m // tm)
    return pl.pallas_call(
        functools.partial(_swiglu_body, layer, nb, row_groups), grid=(nb, m // tm),
        in_specs=[pl.BlockSpec((tm, k), lambda j, i: (i, 0)),
                  pl.BlockSpec(memory_space=pl.ANY), c_in],
        out_specs=[pl.BlockSpec((tm, tn), lambda j, i: (i, j)), c_out],
        out_shape=[jax.ShapeDtypeStruct((m, dff), BF16), jax.ShapeDtypeStruct(cast.shape[1:], BF16)],
        scratch_shapes=[pltpu.VMEM((2, k, tn), F32), pltpu.VMEM((k, 2 * tn), BF16),
                        pltpu.SemaphoreType.DMA((2,))],
        compiler_params=_params(("arbitrary", "arbitrary"), VMEM_LIMIT), name="swiglu_up",
    )(a, w_gu, cast)


def _conv_body(n_prompt_tiles, group, cb_ref, cc_ref, cx_ref, pc_ref, px_ref, cache_ref, w_ref,
               y_ref, tail_ref):
    t = pl.program_id(0)
    u = cc_ref[...] * cx_ref[...]
    rows = u.shape[0]
    row = lax.broadcasted_iota(jnp.int32, u.shape, 0)
    u1 = pltpu.roll(u, 1, 0)
    u2 = pltpu.roll(u, 2, 0)
    w = w_ref[...]

    def finish(um1, um2):
        y = um2 * w[0:1] + um1 * w[1:2] + u * w[2:3]
        y_ref[...] = (cb_ref[...] * y).astype(y_ref.dtype)

    @pl.when(t < n_prompt_tiles)
    def _():
        prev = jnp.where(t > 0, pc_ref[...] * px_ref[...], 0.0)
        finish(jnp.where(row == 0, prev[7:8], u1),
               jnp.where(row == 0, prev[6:7], jnp.where(row == 1, prev[7:8], u2)))

    @pl.when(t >= n_prompt_tiles)
    def _():
        um1, um2 = u1, u2
        for g in range(rows // group):
            c0 = cache_ref[g, 0:1]
            c1 = cache_ref[g, 1:2]
            um1 = jnp.where(row == g * group, c1, um1)
            um2 = jnp.where(row == g * group, c0, jnp.where(row == g * group + 1, c1, um2))
        finish(um1, um2)

    for g in range(rows // group):
        tail_ref[g] = u[(g + 1) * group - 8:(g + 1) * group]


def gated_conv(z, cache, w, n_prompt_rows, group=CHUNK, tr=512):
    m = z.shape[0]
    c = w.shape[1]
    gpt = tr // group
    npt = n_prompt_rows // tr
    assert n_prompt_rows % tr == 0 and m % tr == 0 and tr % group == 0
    row_spec = lambda col: pl.BlockSpec((tr, c), lambda t: (t, col))
    prev_spec = lambda col: pl.BlockSpec((8, c), lambda t: (jnp.maximum(t * (tr // 8) - 1, 0), col))
    return pl.pallas_call(
        functools.partial(_conv_body, npt, group), grid=(m // tr,),
        in_specs=[row_spec(0), row_spec(1), row_spec(2), prev_spec(1), prev_spec(2),
                  pl.BlockSpec((gpt, 2, c), lambda t: (jnp.maximum(t - npt, 0), 0, 0)),
                  pl.BlockSpec((3, c), lambda t: (0, 0))],
        out_specs=[pl.BlockSpec((tr, c), lambda t: (t, 0)),
                   pl.BlockSpec((gpt, 8, c), lambda t: (t, 0, 0))],
        out_shape=[jax.ShapeDtypeStruct((m, c), BF16), jax.ShapeDtypeStruct((m // group, 8, c), F32)],
        compiler_params=_params(("parallel",), VMEM_LIMIT), name="gated_conv",
    )(z, z, z, z, z, cache, w)


def _qk_body(heads, aq_ref, ak_ref, av_ref, gq_ref, gk_ref, qn_ref, kn_ref, kpad_ref, vpad_ref):
    s = pl.program_id(0)

    @pl.when(s == 0)
    def _():
        kpad_ref[...] = jnp.zeros_like(kpad_ref)
        vpad_ref[...] = jnp.zeros_like(vpad_ref)

    @pl.when(s > 0)
    def _():
        dh = gq_ref.shape[1]
        gq = gq_ref[...]
        gk = gk_ref[...]
        for h in range(heads):
            sl = slice(h * dh, (h + 1) * dh)
            qn_ref[:, sl] = _rms(aq_ref[:, sl], gq).astype(qn_ref.dtype)
            kn = _rms(ak_ref[:, sl], gk)
            kn_ref[:, sl] = kn
            kpad_ref[:, sl] = kn.astype(kpad_ref.dtype)
        vpad_ref[...] = av_ref[...].astype(vpad_ref.dtype)


def qk_norm(z, g_q, g_k, col0, heads, tr=WINDOW):
    m = z.shape[0]
    dh = g_q.shape[0]
    c = heads * dh
    assert tr == WINDOW and m % tr == 0
    tile = lambda s: jnp.maximum(s - 1, 0)
    spec = lambda col: pl.BlockSpec((tr, c), lambda s: (tile(s), col))
    g_spec = pl.BlockSpec((1, dh), lambda s: (0, 0))
    o_spec = pl.BlockSpec((tr, c), lambda s: (tile(s), 0))
    pad_spec = pl.BlockSpec((tr, c), lambda s: (s, 0))
    return pl.pallas_call(
        functools.partial(_qk_body, heads), grid=(1 + m // tr,),
        in_specs=[spec(col0), spec(col0 + 1), spec(col0 + 2), g_spec, g_spec],
        out_specs=[o_spec, o_spec, pad_spec, pad_spec],
        out_shape=[jax.ShapeDtypeStruct((m, c), BF16), jax.ShapeDtypeStruct((m, c), F32),
                   jax.ShapeDtypeStruct((WINDOW + m, c), BF16), jax.ShapeDtypeStruct((WINDOW + m, c), BF16)],
        compiler_params=_params(("arbitrary",), VMEM_LIMIT), name="qk_norm",
    )(z, z, z, g_q.reshape(1, dh), g_k.reshape(1, dh))


def _toeplitz_bias(vec, n_queries, n_keys):
    lanes = vec.shape[-1]
    return pltpu.roll(jnp.broadcast_to(vec, (n_queries, lanes)), 0, 1, stride=1, stride_axis=0)[:, :n_keys]


def _softmax_pv(s, v):
    p = jnp.exp(s - jnp.max(s, axis=-1, keepdims=True))
    p = p * (1.0 / jnp.sum(p, axis=-1, keepdims=True))
    return jnp.dot(p.astype(v.dtype), v, preferred_element_type=F32)


def _prompt_attn_body(chunks_per_tile, scale, q_ref, k_ref, v_ref, tbias_ref, o_ref, tile_bias_ref):
    t = pl.program_id(1)
    tr = chunks_per_tile * CHUNK
    tk = tr + WINDOW

    @pl.when(t == 0)
    def _():
        row = lax.broadcasted_iota(jnp.int32, (tr, tk), 0)
        col = lax.broadcasted_iota(jnp.int32, (tr, tk), 1)
        first = jnp.left_shift(jnp.right_shift(row, CHUNK.bit_length() - 1), CHUNK.bit_length() - 1)
        in_band = jnp.logical_and(col >= first, col < first + SPAN)
        tile_bias_ref[...] = jnp.where(in_band, _toeplitz_bias(tbias_ref[0], tr, tk), NEG_INF)

    start = pl.multiple_of(t * tr, tr)
    k = k_ref[pl.ds(start, tk), :]
    v = v_ref[pl.ds(start, tk), :]
    rg = tr // PROMPT_ROW_GROUPS
    col = lax.broadcasted_iota(jnp.int32, (rg, tk), 1)
    for r in range(0, tr, rg):
        s = lax.dot_general(q_ref[r:r + rg, :], k, NT_DIMS, preferred_element_type=F32)
        s = s * scale + tile_bias_ref[r:r + rg, :]
        s = jnp.where(col >= WINDOW - t * tr, s, NEG_INF)
        o_ref[r:r + rg, :] = _softmax_pv(s, v).astype(o_ref.dtype)


def prompt_attention(qn, keys, vals, rel_bias, n_prompt_rows, chunks_per_tile=8):
    c = qn.shape[1]
    heads = rel_bias.shape[0]
    dh = c // heads
    tr = CHUNK * chunks_per_tile
    tk = tr + WINDOW
    kr = keys.shape[0]
    assert n_prompt_rows % tr == 0 and kr >= n_prompt_rows + WINDOW
    return pl.pallas_call(
        functools.partial(_prompt_attn_body, chunks_per_tile, dh ** -0.5),
        grid=(heads, n_prompt_rows // tr),
        in_specs=[pl.BlockSpec((tr, dh), lambda h, t: (t, h)),
                  pl.BlockSpec((kr, dh), lambda h, t: (0, h)),
                  pl.BlockSpec((kr, dh), lambda h, t: (0, h)),
                  pl.BlockSpec((1, 1, tk + tr), lambda h, t: (h, 0, 0))],
        out_specs=pl.BlockSpec((tr, dh), lambda h, t: (t, h)),
        out_shape=jax.ShapeDtypeStruct((n_prompt_rows, c), BF16),
        scratch_shapes=[pltpu.VMEM((tr, tk), F32)],
        compiler_params=_params(("parallel", "arbitrary"), VMEM_LIMIT), name="prompt_attention",
    )(qn, keys, vals, rel_bias_rows(rel_bias, tr, tk))


def _sample_attn_body(heads, scale, q_ref, kn_ref, vn_ref, ck_ref, cv_ref, bias_ref, o_ref):
    dh = q_ref.shape[1] // heads
    for h in range(heads):
        sl = slice(h * dh, (h + 1) * dh)
        k = jnp.concatenate([ck_ref[pl.ds(h, WINDOW, stride=heads), :].astype(kn_ref.dtype), kn_ref[:, sl]], axis=0)
        v = jnp.concatenate([cv_ref[pl.ds(h, WINDOW, stride=heads), :].astype(vn_ref.dtype), vn_ref[:, sl]], axis=0)
        s = lax.dot_general(q_ref[:, sl], k, NT_DIMS, preferred_element_type=F32)
        s = s * scale + _toeplitz_bias(bias_ref[h], CHUNK, SPAN)
        o_ref[:, sl] = _softmax_pv(s, v).astype(o_ref.dtype)


def sample_attention(qn, keys, vals, cache_k, cache_v, layer, rel_bias, n_prompt_rows):
    m, c = qn.shape
    depth, nseq, win, heads, dh = cache_k.shape
    assert win == WINDOW and heads * dh == c and m - n_prompt_rows == nseq * CHUNK
    q0 = n_prompt_rows // CHUNK
    k0 = (n_prompt_rows + WINDOW) // CHUNK
    cache_spec = pl.BlockSpec((None, None, WINDOW * heads, dh), lambda n: (layer, n, 0, 0))
    return pl.pallas_call(
        functools.partial(_sample_attn_body, heads, dh ** -0.5), grid=(nseq,),
        in_specs=[pl.BlockSpec((CHUNK, c), lambda n: (q0 + n, 0)),
                  pl.BlockSpec((CHUNK, c), lambda n: (k0 + n, 0)),
                  pl.BlockSpec((CHUNK, c), lambda n: (k0 + n, 0)),
                  cache_spec, cache_spec,
                  pl.BlockSpec((heads, 1, SPAN + CHUNK), lambda n: (0, 0, 0))],
        out_specs=pl.BlockSpec((CHUNK, c), lambda n: (n, 0)),
        out_shape=jax.ShapeDtypeStruct((nseq * CHUNK, c), BF16),
        compiler_params=_params(("parallel",), VMEM_LIMIT), name="sample_attention",
    )(qn, keys, vals, cache_k.reshape(depth, nseq, WINDOW * heads, dh),
      cache_v.reshape(depth, nseq, WINDOW * heads, dh), rel_bias_rows(rel_bias, CHUNK, SPAN))


def rel_bias_rows(rel_bias, n_queries, n_keys):
    lanes = n_keys + n_queries
    i = jnp.arange(lanes)
    dist = jnp.where(i < n_keys, -i, lanes - i)
    rel = jnp.clip(dist + WINDOW, -REL_CLIP, REL_CLIP) + REL_CLIP
    return rel_bias.astype(F32)[:, rel].reshape(rel_bias.shape[0], 1, lanes)


def _gla_body(n_prompt_tiles, seg_rows, lq_ref, lk_ref, lv_ref, lr_ref, la_ref, wa2_ref, ba_ref, g_ref,
              s0_ref, y_ref, soutp_ref, souts_ref, state_ref, o_ref):
    t = pl.program_id(0)
    rows = lq_ref.shape[0]
    heads, dv, dk = state_ref.shape
    cb = GLA_CHUNK
    nc = rows // cb
    chunks_per_seg = seg_rows // cb
    nseg = rows // seg_rows
    is_sample = t >= n_prompt_tiles
    shift = cb.bit_length() - 1

    r_i = lax.broadcasted_iota(jnp.int32, (rows, rows), 0)
    c_i = lax.broadcasted_iota(jnp.int32, (rows, rows), 1)
    same_chunk = jnp.right_shift(r_i, shift) == jnp.right_shift(c_i, shift)
    chunk_tri = jnp.logical_and(same_chunk, r_i >= c_i).astype(F32)
    causal = (lax.broadcasted_iota(jnp.int32, (cb, cb), 0)
              >= lax.broadcasted_iota(jnp.int32, (cb, cb), 1))[None]

    kd = heads * dk
    x = jnp.dot(la_ref[...], wa2_ref[...], precision=HIGHEST, preferred_element_type=F32) + ba_ref[...]
    log_a = (jnp.minimum(x, 0.0) - jnp.log1p(jnp.exp(-jnp.abs(x)))) * (1.0 / GLA_TAU)
    b = jnp.dot(chunk_tri, log_a, precision=HIGHEST, preferred_element_type=F32).reshape(nc, cb, kd)
    b_mid = b[:, cb // 2 - 1:cb // 2]
    b_last = b[:, cb - 1:cb]
    q = (lq_ref[...] * (dk ** -0.5)).reshape(nc, cb, kd)
    k = lk_ref[...].reshape(nc, cb, kd)
    q_dec_all = (q * jnp.exp(b)).astype(BF16)
    q_mid_all = (q * jnp.exp(b - b_mid)).astype(BF16)
    k_mid_all = (k * jnp.exp(b_mid - b)).astype(BF16)
    k_rem_all = (k * jnp.exp(b_last - b)).astype(BF16)
    decay_all = jnp.exp(b_last)
    pre = []
    for h in range(heads):
        ks = slice(h * dk, (h + 1) * dk)
        v = lv_ref[:, h * dv:(h + 1) * dv].astype(BF16).reshape(nc, cb, dv)
        att = jnp.einsum("clk,cmk->clm", q_mid_all[:, :, ks], k_mid_all[:, :, ks],
                         preferred_element_type=F32)
        att = jnp.where(causal, att, 0.0).astype(BF16)
        o_intra = jnp.einsum("clm,cmv->clv", att, v, preferred_element_type=F32)
        pre.append((q_dec_all[:, :, ks], k_rem_all[:, :, ks], v, o_intra, decay_all[:, :, ks]))

    for s in range(nseg):
        @pl.when(is_sample)
        def _():
            for h in range(heads):
                state_ref[h] = s0_ref[s, h].T

        if s == 0:
            @pl.when(t == 0)
            def _():
                state_ref[...] = jnp.zeros_like(state_ref)

        for c in range(s * chunks_per_seg, (s + 1) * chunks_per_seg):
            for h in range(heads):
                q_dec, k_rem, v, o_intra, decay = pre[h]
                state = state_ref[h]
                o = o_intra[c] + lax.dot_general(q_dec[c], state.astype(BF16), NT_DIMS,
                                                 preferred_element_type=F32)
                kv = lax.dot_general(v[c], k_rem[c], TN_DIMS, preferred_element_type=F32)
                state_ref[h] = decay[c] * state + kv
                o_ref[c * cb:(c + 1) * cb, h * dv:(h + 1) * dv] = o

        @pl.when(is_sample)
        def _():
            for h in range(heads):
                souts_ref[s, h] = state_ref[h].T

        if s == nseg - 1:
            @pl.when(t == n_prompt_tiles - 1)
            def _():
                for h in range(heads):
                    soutp_ref[0, h] = state_ref[h].T

    g = g_ref[...]
    for h in range(heads):
        vs = slice(h * dv, (h + 1) * dv)
        r = lr_ref[:, vs]
        y_ref[:, vs] = (_rms(o_ref[:, vs], g) * (r * jax.nn.sigmoid(r))).astype(y_ref.dtype)


def gla(z, la, w_a2, b_a, g_gla, s0, layer, n_prompt_rows, seg_rows, col_q, col_k, col_v, col_r, tr=128):
    m = z.shape[0]
    _, nseq, heads, dk, dv = s0.shape
    kd, vd = heads * dk, heads * dv
    npt = n_prompt_rows // tr
    spt = tr // seg_rows
    assert n_prompt_rows % tr == 0 and tr % seg_rows == 0 and seg_rows % GLA_CHUNK == 0
    sample_tile = lambda t: (jnp.maximum(t - npt, 0), 0, 0, 0)
    wa2p = jnp.zeros((LA_PAD, kd), F32).at[:w_a2.shape[0]].set(w_a2)
    return pl.pallas_call(
        functools.partial(_gla_body, npt, seg_rows), grid=(m // tr,),
        in_specs=[pl.BlockSpec((tr, kd), lambda t: (t, col_q // kd)),
                  pl.BlockSpec((tr, kd), lambda t: (t, col_k // kd)),
                  pl.BlockSpec((tr, vd), lambda t: (t, col_v // vd)),
                  pl.BlockSpec((tr, vd), lambda t: (t, col_r // vd)),
                  pl.BlockSpec((tr, LA_PAD), lambda t: (t, 0)),
                  pl.BlockSpec((LA_PAD, kd), lambda t: (0, 0)),
                  pl.BlockSpec((1, kd), lambda t: (0, 0)),
                  pl.BlockSpec((1, dv), lambda t: (0, 0)),
                  pl.BlockSpec((None, spt, heads, dk, dv), lambda t: (layer,) + sample_tile(t))],
        out_specs=[pl.BlockSpec((tr, vd), lambda t: (t, 0)),
                   pl.BlockSpec((1, heads, dk, dv), lambda t: (0, 0, 0, 0)),
                   pl.BlockSpec((spt, heads, dk, dv), sample_tile)],
        out_shape=[jax.ShapeDtypeStruct((m, vd), BF16),
                   jax.ShapeDtypeStruct((1, heads, dk, dv), F32),
                   jax.ShapeDtypeStruct((nseq, heads, dk, dv), F32)],
        scratch_shapes=[pltpu.VMEM((heads, dv, dk), F32), pltpu.VMEM((tr, vd), F32)],
        compiler_params=_params(("arbitrary",), VMEM_LIMIT), name="gla",
    )(z, z, z, z, la, wa2p, b_a.reshape(1, -1), g_gla.reshape(1, -1), s0)


def _branch_body(n_prompt_tiles, ya_ref, ybp_ref, ybs_ref, yc_ref, wa_ref, wb_ref, wc_ref,
                 ga_ref, gb_ref, gc_ref, o_ref):
    yb = _pick_rows(pl.program_id(0), n_prompt_tiles, ybp_ref, ybs_ref)
    m = jax.nn.sigmoid(ga_ref[...]) * jnp.dot(ya_ref[...], wa_ref[...], preferred_element_type=F32)
    m = m + jax.nn.sigmoid(gb_ref[...]) * jnp.dot(yb, wb_ref[...], preferred_element_type=F32)
    m = m + jax.nn.sigmoid(gc_ref[...]) * jnp.dot(yc_ref[...], wc_ref[...], preferred_element_type=F32)
    o_ref[...] = m.astype(o_ref.dtype)


def branch_merge(ya, yb_p, yb_s, yc, w_branch, layer, z, gate_col, tm=1024, tn=512):
    m = ya.shape[0]
    ca, cb, cc = ya.shape[1], yb_p.shape[1], yc.shape[1]
    mp = yb_p.shape[0]
    d = w_branch.shape[2]
    assert ca == cb and cc % ca == 0 and (ca + cb) % cc == 0
    assert mp % tm == 0 and yb_s.shape[0] % tm == 0 and mp + yb_s.shape[0] == m
    gb = gate_col // tn
    nd = d // tn
    ybp_spec, ybs_spec = _split_row_specs(tm, cb, mp // tm, 0, lambda i, j: 0)
    return pl.pallas_call(
        functools.partial(_branch_body, mp // tm), grid=(m // tm, nd),
        in_specs=[pl.BlockSpec((tm, ca), lambda i, j: (i, 0)),
                  ybp_spec, ybs_spec,
                  pl.BlockSpec((tm, cc), lambda i, j: (i, 0)),
                  pl.BlockSpec((None, ca, tn), lambda i, j: (layer, 0, j)),
                  pl.BlockSpec((None, cb, tn), lambda i, j: (layer, 1, j)),
                  pl.BlockSpec((None, cc, tn), lambda i, j: (layer, (ca + cb) // cc, j)),
                  pl.BlockSpec((tm, tn), lambda i, j: (i, gb + j)),
                  pl.BlockSpec((tm, tn), lambda i, j: (i, gb + nd + j)),
                  pl.BlockSpec((tm, tn), lambda i, j: (i, gb + 2 * nd + j))],
        out_specs=pl.BlockSpec((tm, tn), lambda i, j: (i, j)),
        out_shape=jax.ShapeDtypeStruct((m, d), BF16),
        compiler_params=_params(("parallel", "parallel"), VMEM_LIMIT), name="branch_merge",
    )(ya, yb_p, yb_s, yc, w_branch, w_branch, w_branch, z, z, z)


def kernel(x_prompt, x_sample, cache_conv, cache_k, cache_v, state_gla, g_mix, w_in, conv_w, g_q, g_k,
           rel_bias, w_a2, b_a, g_gla, w_branch, w_out, g_ffn, w_gu, w_down):
    nb, seq, d = x_prompt.shape
    nd, ds, _ = x_sample.shape
    depth = g_mix.shape[0]
    assert nb == 1 and ds == CHUNK
    mp = nb * seq
    ms = nd * ds
    conv_dim = conv_w.shape[2]
    att_dim = ATT_HEADS * ATT_HEAD_DIM
    kdim = GLA_HEADS * GLA_DK
    vdim = GLA_HEADS * GLA_DV
    col_att = 3 * conv_dim
    col_lq = col_att + 3 * att_dim
    col_lk = col_lq + kdim
    col_lv = col_lk + kdim
    col_lr = col_lv + vdim
    col_la = col_lr + vdim
    col_gate = col_la
    keep = min(WINDOW, seq)

    xp = x_prompt.reshape(mp, d)
    xs = x_sample.reshape(ms, d)
    w_in_t = jnp.swapaxes(w_in, 1, 2)
    outs = {name: [] for name in ("conv_p", "k_p", "v_p", "gla_p", "conv_s", "k_s", "v_s", "gla_s")}
    for l in range(depth):
        h, la = rmsnorm_la(xp, xs, g_mix[l], w_in_t, l, col_la, GLA_RANK)
        z, w_branch_b, w_out_b = in_proj(h, w_in_t, l, col_la, GLA_RANK, w_branch, w_out)

        y_a, conv_tail = gated_conv(z, cache_conv[l], conv_w[l], mp)

        qn, kn, keys, vals = qk_norm(z, g_q[l], g_k[l], col_att // att_dim, ATT_HEADS)
        yb_p = prompt_attention(qn, keys, vals, rel_bias[l], mp)
        yb_s = sample_attention(qn, keys, vals, cache_k, cache_v, l, rel_bias[l], mp)

        y_c, s_prompt, s_sample = gla(z, la, w_a2[l], b_a[l], g_gla[l], state_gla, l, mp, ds,
                                      col_lq, col_lk, col_lv, col_lr)

        mrg = branch_merge(y_a, yb_p, yb_s, y_c, w_branch_b[None], 0, z, col_gate)
        xp, xs = matmul_residual(mrg, w_out_b[None], 0, xp, xs)

        hf = rmsnorm_bf16(xp, xs, g_ffn[l])
        hid, w_down_b = swiglu_up(hf, w_gu, l, w_down)
        xp, xs = matmul_residual(hid, w_down_b[None], 0, xp, xs, tm=512, tn=512)

        v_new = z[:, col_att + 2 * att_dim:col_att + 3 * att_dim]
        outs["conv_p"].append(conv_tail[mp // CHUNK - 1, 6:8][None])
        outs["conv_s"].append(conv_tail[mp // CHUNK:, 6:8])
        outs["k_p"].append(kn[mp - keep:mp].reshape(nb, keep, ATT_HEADS, ATT_HEAD_DIM))
        outs["v_p"].append(v_new[mp - keep:mp].reshape(nb, keep, ATT_HEADS, ATT_HEAD_DIM))
        outs["k_s"].append(kn[mp:].reshape(nd, ds, ATT_HEADS, ATT_HEAD_DIM))
        outs["v_s"].append(v_new[mp:].reshape(nd, ds, ATT_HEADS, ATT_HEAD_DIM))
        outs["gla_p"].append(s_prompt)
        outs["gla_s"].append(s_sample)

    st = {name: jnp.stack(v) for name, v in outs.items()}
    return (xp.reshape(nb, seq, d), xs.reshape(nd, ds, d),
            st["conv_p"], st["k_p"], st["v_p"], st["gla_p"],
            st["conv_s"], st["k_s"], st["v_s"], st["gla_s"])
```

```python
import functools

import jax
import jax.numpy as jnp
from jax import lax
from jax.experimental import pallas as pl
from jax.experimental.pallas import tpu as pltpu

F32 = jnp.float32
BF16 = jnp.bfloat16
HIGHEST = lax.Precision.HIGHEST

CHUNK = 64
BAND_CHUNKS = 8
WINDOW = BAND_CHUNKS * CHUNK
SPAN = WINDOW + CHUNK
REL_CLIP = 128
ATT_HEADS = 8
ATT_HEAD_DIM = 128
GLA_HEADS = 4
GLA_DK = 256
GLA_DV = 512
GLA_RANK = 16
GLA_TAU = 16.0
GLA_CHUNK = 32
LA_PAD = 128
PROMPT_ROW_GROUPS = 2
NEG_INF = -1e30
EPS = 1e-6
VMEM_LIMIT = 56 * 1024 * 1024
IN_PROJ_VMEM_LIMIT = 60 * 1024 * 1024

NT_DIMS = (((1,), (1,)), ((), ()))
TN_DIMS = (((0,), (0,)), ((), ()))


def _params(sem, vmem=None):
    return pltpu.CompilerParams(dimension_semantics=sem, vmem_limit_bytes=vmem)


def _rms(x, g):
    return x * lax.rsqrt(jnp.mean(x * x, axis=-1, keepdims=True) + EPS) * g


def _split_row_specs(tr, tn, n_prompt_tiles, row_axis, col_of):
    def prompt_map(*ids):
        return (jnp.minimum(ids[row_axis], n_prompt_tiles - 1), col_of(*ids))

    def sample_map(*ids):
        return (jnp.maximum(ids[row_axis] - n_prompt_tiles, 0), col_of(*ids))

    return pl.BlockSpec((tr, tn), prompt_map), pl.BlockSpec((tr, tn), sample_map)


def _pick_rows(i, n_prompt_tiles, p_ref, s_ref):
    return jnp.where(i < n_prompt_tiles, p_ref[...], s_ref[...])


def _norm_body(n_prompt_tiles, xp_ref, xs_ref, g_ref, h_ref):
    x = _pick_rows(pl.program_id(0), n_prompt_tiles, xp_ref, xs_ref)
    h_ref[...] = _rms(x, g_ref[...]).astype(h_ref.dtype)


def _norm_la_body(n_prompt_tiles, xp_ref, xs_ref, g_ref, wla_ref, h_ref, la_ref, wlab_ref):
    rank = wla_ref.shape[0]

    @pl.when(pl.program_id(0) == 0)
    def _():
        wlab_ref[...] = jnp.zeros_like(wlab_ref)
        wlab_ref[0:rank, :] = wla_ref[...].astype(wlab_ref.dtype)

    x = _pick_rows(pl.program_id(0), n_prompt_tiles, xp_ref, xs_ref)
    h = _rms(x, g_ref[...]).astype(h_ref.dtype)
    h_ref[...] = h
    la_ref[...] = lax.dot_general(h, wlab_ref[...], NT_DIMS, preferred_element_type=F32)


def rmsnorm_bf16(xp, xs, g, tr=512):
    (mp, d), ms = xp.shape, xs.shape[0]
    assert mp % tr == 0 and ms % tr == 0
    p_spec, s_spec = _split_row_specs(tr, d, mp // tr, 0, lambda i: 0)
    return pl.pallas_call(
        functools.partial(_norm_body, mp // tr), grid=((mp + ms) // tr,),
        in_specs=[p_spec, s_spec, pl.BlockSpec((1, d), lambda i: (0, 0))],
        out_specs=pl.BlockSpec((tr, d), lambda i: (i, 0)),
        out_shape=jax.ShapeDtypeStruct((mp + ms, d), BF16),
        compiler_params=_params(("parallel",), VMEM_LIMIT), name="rmsnorm",
    )(xp, xs, g.reshape(1, d))


def rmsnorm_la(xp, xs, g, w_t, layer, la_row, rank, tr=512):
    (mp, d), ms = xp.shape, xs.shape[0]
    assert mp % tr == 0 and ms % tr == 0
    m = mp + ms
    p_spec, s_spec = _split_row_specs(tr, d, mp // tr, 0, lambda i: 0)
    return pl.pallas_call(
        functools.partial(_norm_la_body, mp // tr), grid=(m // tr,),
        in_specs=[p_spec, s_spec, pl.BlockSpec((1, d), lambda i: (0, 0)),
                  pl.BlockSpec((None, rank, d), lambda i: (layer, la_row // rank, 0))],
        out_specs=[pl.BlockSpec((tr, d), lambda i: (i, 0)), pl.BlockSpec((tr, LA_PAD), lambda i: (i, 0))],
        out_shape=[jax.ShapeDtypeStruct((m, d), BF16), jax.ShapeDtypeStruct((m, LA_PAD), F32)],
        scratch_shapes=[pltpu.VMEM((LA_PAD, d), BF16)],
        compiler_params=_params(("arbitrary",), VMEM_LIMIT), name="rmsnorm_la",
    )(xp, xs, g.reshape(1, d), w_t)


CAST_ROWS = 512


def _in_proj_body(layer, n_aligned, skip, a_ref, w_hbm, c1_ref, c2_ref, o_ref, c1b_ref, c2b_ref,
                  wf_ref, wb_ref, sem):
    j = pl.program_id(0)
    i = pl.program_id(1)
    tn = wb_ref.shape[0]

    c1b_ref[...] = c1_ref[...].astype(c1b_ref.dtype)
    c2b_ref[...] = c2_ref[...].astype(c2b_ref.dtype)

    def weight_copy(jj):
        row0 = pl.multiple_of(jj * tn + jnp.where(jj >= n_aligned, skip, 0), 16)
        return pltpu.make_async_copy(w_hbm.at[layer, pl.ds(row0, tn), :], wf_ref, sem)

    @pl.when(jnp.logical_and(j == 0, i == 0))
    def _():
        weight_copy(0).start()

    @pl.when(i == 0)
    def _():
        weight_copy(j).wait()
        for r in range(0, tn, 128):
            wb_ref[r:r + 128, :] = wf_ref[r:r + 128, :].astype(wb_ref.dtype)

        @pl.when(j + 1 < pl.num_programs(0))
        def _():
            weight_copy(j + 1).start()

    o_ref[...] = lax.dot_general(a_ref[...], wb_ref[...], NT_DIMS, preferred_element_type=F32)


def _ride_along_spec(shape, rows, layer, steps_per_outer):
    last = shape[1] // rows - 1
    return (pl.BlockSpec((None, rows, shape[2]),
                         lambda j, i: (layer, jnp.minimum(j * steps_per_outer + i, last), 0)),
            pl.BlockSpec((rows, shape[2]), lambda j, i: (jnp.minimum(j * steps_per_outer + i, last), 0)))


def in_proj(a, w_t, layer, skip_row, skip, cast1, cast2, tm=1024, tn=1024, cast_rows=32):
    m, k = a.shape
    n = w_t.shape[1] - skip
    steps = (n // tn) * (m // tm)
    assert skip_row % tn == 0 and n % tn == 0 and skip % 16 == 0
    for c in (cast1, cast2):
        assert c.shape[1] % cast_rows == 0 and c.shape[1] // cast_rows <= steps
    c1_in, c1_out = _ride_along_spec(cast1.shape, cast_rows, layer, m // tm)
    c2_in, c2_out = _ride_along_spec(cast2.shape, cast_rows, layer, m // tm)
    return pl.pallas_call(
        functools.partial(_in_proj_body, layer, skip_row // tn, skip), grid=(n // tn, m // tm),
        in_specs=[pl.BlockSpec((tm, k), lambda j, i: (i, 0)),
                  pl.BlockSpec(memory_space=pl.ANY), c1_in, c2_in],
        out_specs=[pl.BlockSpec((tm, tn), lambda j, i: (i, j)), c1_out, c2_out],
        out_shape=[jax.ShapeDtypeStruct((m, n), F32),
                   jax.ShapeDtypeStruct(cast1.shape[1:], BF16), jax.ShapeDtypeStruct(cast2.shape[1:], BF16)],
        scratch_shapes=[pltpu.VMEM((tn, k), F32), pltpu.VMEM((tn, k), BF16), pltpu.SemaphoreType.DMA(())],
        compiler_params=_params(("arbitrary", "arbitrary"), IN_PROJ_VMEM_LIMIT), name="in_proj",
    )(a, w_t, cast1, cast2)


def _mm_res_body(n_prompt_tiles, a_ref, b_ref, rp_ref, rs_ref, op_ref, os_ref):
    i = pl.program_id(1)
    acc = jnp.dot(a_ref[...], b_ref[...], preferred_element_type=F32)

    @pl.when(i < n_prompt_tiles)
    def _():
        op_ref[...] = rp_ref[...] + acc

    @pl.when(i >= n_prompt_tiles)
    def _():
        os_ref[...] = rs_ref[...] + acc


def matmul_residual(a, b, layer, res_p, res_s, tm=1024, tn=512):
    m, k = a.shape
    n = b.shape[2]
    mp, ms = res_p.shape[0], res_s.shape[0]
    assert mp % tm == 0 and ms % tm == 0 and mp + ms == m
    p_spec, s_spec = _split_row_specs(tm, tn, mp // tm, 1, lambda j, i: j)
    return pl.pallas_call(
        functools.partial(_mm_res_body, mp // tm), grid=(n // tn, m // tm),
        in_specs=[pl.BlockSpec((tm, k), lambda j, i: (i, 0)),
                  pl.BlockSpec((None, k, tn), lambda j, i: (layer, 0, j)),
                  p_spec, s_spec],
        out_specs=[p_spec, s_spec],
        out_shape=[jax.ShapeDtypeStruct((mp, n), F32), jax.ShapeDtypeStruct((ms, n), F32)],
        compiler_params=_params(("parallel", "arbitrary"), VMEM_LIMIT), name="matmul_residual",
    )(a, b, res_p, res_s)


def _swiglu_body(layer, nb, row_groups, a_ref, w_hbm, c_ref, o_ref, cb_ref, wf_ref, wb_ref, sem):
    j = pl.program_id(0)
    i = pl.program_id(1)
    k, tn = wf_ref.shape[1], wf_ref.shape[2]

    cb_ref[...] = c_ref[...].astype(cb_ref.dtype)

    def weight_copies(jj):
        return [pltpu.make_async_copy(w_hbm.at[layer, :, pl.ds(pl.multiple_of((jj + half * nb) * tn, tn), tn)],
                                      wf_ref.at[half], sem.at[half]) for half in range(2)]

    @pl.when(jnp.logical_and(j == 0, i == 0))
    def _():
        for c in weight_copies(0):
            c.start()

    @pl.when(i == 0)
    def _():
        for c in weight_copies(j):
            c.wait()

        def step(r, carry):
            rs = pl.ds(pl.multiple_of(r * CAST_ROWS, CAST_ROWS), CAST_ROWS)
            wb_ref[rs, 0:tn] = wf_ref[0, rs, :].astype(wb_ref.dtype)
            wb_ref[rs, tn:2 * tn] = wf_ref[1, rs, :].astype(wb_ref.dtype)
            return carry
        lax.fori_loop(0, k // CAST_ROWS, step, 0)

        @pl.when(j + 1 < pl.num_programs(0))
        def _():
            for c in weight_copies(j + 1):
                c.start()

    rg = a_ref.shape[0] // row_groups
    for r in range(0, a_ref.shape[0], rg):
        u = jnp.dot(a_ref[r:r + rg, :], wb_ref[...], preferred_element_type=F32)
        ug = u[:, 0:tn]
        o_ref[r:r + rg, :] = (ug * jax.nn.sigmoid(ug) * u[:, tn:2 * tn]).astype(o_ref.dtype)


def swiglu_up(a, w_gu, layer, cast, tm=2304, tn=256, row_groups=2, cast_rows=64):
    m, k = a.shape
    dff = w_gu.shape[2] // 2
    nb = dff // tn
    assert m % tm == 0 and dff % tn == 0 and k % CAST_ROWS == 0 and tm % (8 * row_groups) == 0
    assert cast.shape[1] % cast_rows == 0 and cast.shape[1] // cast_rows <= nb * (m // tm)
    c_in, c_out = _ride_along_spec(cast.shape, cast_rows, layer, m // tm)
    return pl.pallas_call(
        functools.partial(_swiglu_body, layer, nb, row_groups), grid=(nb, m // tm),
        in_specs=[pl.BlockSpec((tm, k), lambda j, i: (i, 0)),
                  pl.BlockSpec(memory_space=pl.ANY), c_in],
        out_specs=[pl.BlockSpec((tm, tn), lambda j, i: (i, j)), c_out],
        out_shape=[jax.ShapeDtypeStruct((m, dff), BF16), jax.ShapeDtypeStruct(cast.shape[1:], BF16)],
        scratch_shapes=[pltpu.VMEM((2, k, tn), F32), pltpu.VMEM((k, 2 * tn), BF16),
                        pltpu.SemaphoreType.DMA((2,))],
        compiler_params=_params(("arbitrary", "arbitrary"), VMEM_LIMIT), name="swiglu_up",
    )(a, w_gu, cast)


def _conv_body(n_prompt_tiles, group, cb_ref, cc_ref, cx_ref, pc_ref, px_ref, cache_ref, w_ref,
               y_ref, tail_ref):
    t = pl.program_id(0)
    u = cc_ref[...] * cx_ref[...]
    rows = u.shape[0]
    row = lax.broadcasted_iota(jnp.int32, u.shape, 0)
    u1 = pltpu.roll(u, 1, 0)
    u2 = pltpu.roll(u, 2, 0)
    w = w_ref[...]

    def finish(um1, um2):
        y = um2 * w[0:1] + um1 * w[1:2] + u * w[2:3]
        y_ref[...] = (cb_ref[...] * y).astype(y_ref.dtype)

    @pl.when(t < n_prompt_tiles)
    def _():
        prev = jnp.where(t > 0, pc_ref[...] * px_ref[...], 0.0)
        finish(jnp.where(row == 0, prev[7:8], u1),
               jnp.where(row == 0, prev[6:7], jnp.where(row == 1, prev[7:8], u2)))

    @pl.when(t >= n_prompt_tiles)
    def _():
        um1, um2 = u1, u2
        for g in range(rows // group):
            c0 = cache_ref[g, 0:1]
            c1 = cache_ref[g, 1:2]
            um1 = jnp.where(row == g * group, c1, um1)
            um2 = jnp.where(row == g * group, c0, jnp.where(row == g * group + 1, c1, um2))
        finish(um1, um2)

    for g in range(rows // group):
        tail_ref[g] = u[(g + 1) * group - 8:(g + 1) * group]


def gated_conv(z, cache, w, n_prompt_rows, group=CHUNK, tr=512):
    m = z.shape[0]
    c = w.shape[1]
    gpt = tr // group
    npt = n_prompt_rows // tr
    assert n_prompt_rows % tr == 0 and m % tr == 0 and tr % group == 0
    row_spec = lambda col: pl.BlockSpec((tr, c), lambda t: (t, col))
    prev_spec = lambda col: pl.BlockSpec((8, c), lambda t: (jnp.maximum(t * (tr // 8) - 1, 0), col))
    return pl.pallas_call(
        functools.partial(_conv_body, npt, group), grid=(m // tr,),
        in_specs=[row_spec(0), row_spec(1), row_spec(2), prev_spec(1), prev_spec(2),
                  pl.BlockSpec((gpt, 2, c), lambda t: (jnp.maximum(t - npt, 0), 0, 0)),
                  pl.BlockSpec((3, c), lambda t: (0, 0))],
        out_specs=[pl.BlockSpec((tr, c), lambda t: (t, 0)),
                   pl.BlockSpec((gpt, 8, c), lambda t: (t, 0, 0))],
        out_shape=[jax.ShapeDtypeStruct((m, c), BF16), jax.ShapeDtypeStruct((m // group, 8, c), F32)],
        compiler_params=_params(("parallel",), VMEM_LIMIT), name="gated_conv",
    )(z, z, z, z, z, cache, w)


def _qk_body(heads, aq_ref, ak_ref, av_ref, gq_ref, gk_ref, qn_ref, kn_ref, kpad_ref, vpad_ref):
    s = pl.program_id(0)

    @pl.when(s == 0)
    def _():
        kpad_ref[...] = jnp.zeros_like(kpad_ref)
        vpad_ref[...] = jnp.zeros_like(vpad_ref)

    @pl.when(s > 0)
    def _():
        dh = gq_ref.shape[1]
        gq = gq_ref[...]
        gk = gk_ref[...]
        for h in range(heads):
            sl = slice(h * dh, (h + 1) * dh)
            qn_ref[:, sl] = _rms(aq_ref[:, sl], gq).astype(qn_ref.dtype)
            kn = _rms(ak_ref[:, sl], gk)
            kn_ref[:, sl] = kn
            kpad_ref[:, sl] = kn.astype(kpad_ref.dtype)
        vpad_ref[...] = av_ref[...].astype(vpad_ref.dtype)


def qk_norm(z, g_q, g_k, col0, heads, tr=WINDOW):
    m = z.shape[0]
    dh = g_q.shape[0]
    c = heads * dh
    assert tr == WINDOW and m % tr == 0
    tile = lambda s: jnp.maximum(s - 1, 0)
    spec = lambda col: pl.BlockSpec((tr, c), lambda s: (tile(s), col))
    g_spec = pl.BlockSpec((1, dh), lambda s: (0, 0))
    o_spec = pl.BlockSpec((tr, c), lambda s: (tile(s), 0))
    pad_spec = pl.BlockSpec((tr, c), lambda s: (s, 0))
    return pl.pallas_call(
        functools.partial(_qk_body, heads), grid=(1 + m // tr,),
        in_specs=[spec(col0), spec(col0 + 1), spec(col0 + 2), g_spec, g_spec],
        out_specs=[o_spec, o_spec, pad_spec, pad_spec],
        out_shape=[jax.ShapeDtypeStruct((m, c), BF16), jax.ShapeDtypeStruct((m, c), F32),
                   jax.ShapeDtypeStruct((WINDOW + m, c), BF16), jax.ShapeDtypeStruct((WINDOW + m, c), BF16)],
        compiler_params=_params(("arbitrary",), VMEM_LIMIT), name="qk_norm",
    )(z, z, z, g_q.reshape(1, dh), g_k.reshape(1, dh))


def _toeplitz_bias(vec, n_queries, n_keys):
    lanes = vec.shape[-1]
    return pltpu.roll(jnp.broadcast_to(vec, (n_queries, lanes)), 0, 1, stride=1, stride_axis=0)[:, :n_keys]


def _softmax_pv(s, v):
    p = jnp.exp(s - jnp.max(s, axis=-1, keepdims=True))
    p = p * (1.0 / jnp.sum(p, axis=-1, keepdims=True))
    return jnp.dot(p.astype(v.dtype), v, preferred_element_type=F32)


def _prompt_attn_body(chunks_per_tile, heads_per_step, scale, q_ref, k_ref, v_ref, tbias_ref, o_ref,
                      tile_bias_ref):
    t = pl.program_id(1)
    tr = chunks_per_tile * CHUNK
    tk = tr + WINDOW
    dh = q_ref.shape[1] // heads_per_step

    @pl.when(t == 0)
    def _():
        row = lax.broadcasted_iota(jnp.int32, (tr, tk), 0)
        col = lax.broadcasted_iota(jnp.int32, (tr, tk), 1)
        first = jnp.left_shift(jnp.right_shift(row, CHUNK.bit_length() - 1), CHUNK.bit_length() - 1)
        in_band = jnp.logical_and(col >= first, col < first + SPAN)
        for hh in range(heads_per_step):
            tile_bias_ref[hh] = jnp.where(in_band, _toeplitz_bias(tbias_ref[hh], tr, tk), NEG_INF)

    start = pl.multiple_of(t * tr, tr)
    rg = tr // PROMPT_ROW_GROUPS
    col = lax.broadcasted_iota(jnp.int32, (rg, tk), 1)
    for hh in range(heads_per_step):
        sl = slice(hh * dh, (hh + 1) * dh)
        k = k_ref[pl.ds(start, tk), sl]
        v = v_ref[pl.ds(start, tk), sl]
        for r in range(0, tr, rg):
            s = lax.dot_general(q_ref[r:r + rg, sl], k, NT_DIMS, preferred_element_type=F32)
            s = s * scale + tile_bias_ref[hh, r:r + rg, :]
            s = jnp.where(col >= WINDOW - t * tr, s, NEG_INF)
            o_ref[r:r + rg, sl] = _softmax_pv(s, v).astype(o_ref.dtype)


def prompt_attention(qn, keys, vals, rel_bias, n_prompt_rows, chunks_per_tile=8, heads_per_step=4):
    c = qn.shape[1]
    heads = rel_bias.shape[0]
    dh = c // heads
    hw = heads_per_step * dh
    tr = CHUNK * chunks_per_tile
    tk = tr + WINDOW
    kr = keys.shape[0]
    assert n_prompt_rows % tr == 0 and kr >= n_prompt_rows + WINDOW and heads % heads_per_step == 0
    return pl.pallas_call(
        functools.partial(_prompt_attn_body, chunks_per_tile, heads_per_step, dh ** -0.5),
        grid=(heads // heads_per_step, n_prompt_rows // tr),
        in_specs=[pl.BlockSpec((tr, hw), lambda h, t: (t, h)),
                  pl.BlockSpec((kr, hw), lambda h, t: (0, h)),
                  pl.BlockSpec((kr, hw), lambda h, t: (0, h)),
                  pl.BlockSpec((heads_per_step, 1, tk + tr), lambda h, t: (h, 0, 0))],
        out_specs=pl.BlockSpec((tr, hw), lambda h, t: (t, h)),
        out_shape=jax.ShapeDtypeStruct((n_prompt_rows, c), BF16),
        scratch_shapes=[pltpu.VMEM((heads_per_step, tr, tk), F32)],
        compiler_params=_params(("parallel", "arbitrary"), VMEM_LIMIT), name="prompt_attention",
    )(qn, keys, vals, rel_bias_rows(rel_bias, tr, tk))


def _sample_attn_body(heads, scale, q_ref, kn_ref, vn_ref, ck_ref, cv_ref, bias_ref, o_ref):
    dh = q_ref.shape[1] // heads
    for h in range(heads):
        sl = slice(h * dh, (h + 1) * dh)
        k = jnp.concatenate([ck_ref[pl.ds(h, WINDOW, stride=heads), :].astype(kn_ref.dtype), kn_ref[:, sl]], axis=0)
        v = jnp.concatenate([cv_ref[pl.ds(h, WINDOW, stride=heads), :].astype(vn_ref.dtype), vn_ref[:, sl]], axis=0)
        s = lax.dot_general(q_ref[:, sl], k, NT_DIMS, preferred_element_type=F32)
        s = s * scale + _toeplitz_bias(bias_ref[h], CHUNK, SPAN)
        o_ref[:, sl] = _softmax_pv(s, v).astype(o_ref.dtype)


def sample_attention(qn, keys, vals, cache_k, cache_v, layer, rel_bias, n_prompt_rows):
    m, c = qn.shape
    depth, nseq, win, heads, dh = cache_k.shape
    assert win == WINDOW and heads * dh == c and m - n_prompt_rows == nseq * CHUNK
    q0 = n_prompt_rows // CHUNK
    k0 = (n_prompt_rows + WINDOW) // CHUNK
    cache_spec = pl.BlockSpec((None, None, WINDOW * heads, dh), lambda n: (layer, n, 0, 0))
    return pl.pallas_call(
        functools.partial(_sample_attn_body, heads, dh ** -0.5), grid=(nseq,),
        in_specs=[pl.BlockSpec((CHUNK, c), lambda n: (q0 + n, 0)),
                  pl.BlockSpec((CHUNK, c), lambda n: (k0 + n, 0)),
                  pl.BlockSpec((CHUNK, c), lambda n: (k0 + n, 0)),
                  cache_spec, cache_spec,
                  pl.BlockSpec((heads, 1, SPAN + CHUNK), lambda n: (0, 0, 0))],
        out_specs=pl.BlockSpec((CHUNK, c), lambda n: (n, 0)),
        out_shape=jax.ShapeDtypeStruct((nseq * CHUNK, c), BF16),
        compiler_params=_params(("parallel",), VMEM_LIMIT), name="sample_attention",
    )(qn, keys, vals, cache_k.reshape(depth, nseq, WINDOW * heads, dh),
      cache_v.reshape(depth, nseq, WINDOW * heads, dh), rel_bias_rows(rel_bias, CHUNK, SPAN))


def rel_bias_rows(rel_bias, n_queries, n_keys):
    lanes = n_keys + n_queries
    i = jnp.arange(lanes)
    dist = jnp.where(i < n_keys, -i, lanes - i)
    rel = jnp.clip(dist + WINDOW, -REL_CLIP, REL_CLIP) + REL_CLIP
    return rel_bias.astype(F32)[:, rel].reshape(rel_bias.shape[0], 1, lanes)


def _gla_body(n_prompt_tiles, seg_rows, lq_ref, lk_ref, lv_ref, lr_ref, la_ref, wa2_ref, ba_ref, g_ref,
              s0_ref, y_ref, soutp_ref, souts_ref, state_ref, o_ref):
    t = pl.program_id(0)
    rows = lq_ref.shape[0]
    heads, dv, dk = state_ref.shape
    cb = GLA_CHUNK
    nc = rows // cb
    chunks_per_seg = seg_rows // cb
    nseg = rows // seg_rows
    is_sample = t >= n_prompt_tiles
    shift = cb.bit_length() - 1

    r_i = lax.broadcasted_iota(jnp.int32, (rows, rows), 0)
    c_i = lax.broadcasted_iota(jnp.int32, (rows, rows), 1)
    same_chunk = jnp.right_shift(r_i, shift) == jnp.right_shift(c_i, shift)
    chunk_tri = jnp.logical_and(same_chunk, r_i >= c_i).astype(F32)
    causal = (lax.broadcasted_iota(jnp.int32, (cb, cb), 0)
              >= lax.broadcasted_iota(jnp.int32, (cb, cb), 1))[None]

    kd = heads * dk
    x = jnp.dot(la_ref[...], wa2_ref[...], precision=HIGHEST, preferred_element_type=F32) + ba_ref[...]
    log_a = (jnp.minimum(x, 0.0) - jnp.log1p(jnp.exp(-jnp.abs(x)))) * (1.0 / GLA_TAU)
    b = jnp.dot(chunk_tri, log_a, precision=HIGHEST, preferred_element_type=F32).reshape(nc, cb, kd)
    b_mid = b[:, cb // 2 - 1:cb // 2]
    b_last = b[:, cb - 1:cb]
    q = (lq_ref[...] * (dk ** -0.5)).reshape(nc, cb, kd)
    k = lk_ref[...].reshape(nc, cb, kd)
    q_dec_all = (q * jnp.exp(b)).astype(BF16)
    q_mid_all = (q * jnp.exp(b - b_mid)).astype(BF16)
    k_mid_all = (k * jnp.exp(b_mid - b)).astype(BF16)
    k_rem_all = (k * jnp.exp(b_last - b)).astype(BF16)
    decay_all = jnp.exp(b_last)
    pre = []
    for h in range(heads):
        ks = slice(h * dk, (h + 1) * dk)
        v = lv_ref[:, h * dv:(h + 1) * dv].astype(BF16).reshape(nc, cb, dv)
        att = jnp.einsum("clk,cmk->clm", q_mid_all[:, :, ks], k_mid_all[:, :, ks],
                         preferred_element_type=F32)
        att = jnp.where(causal, att, 0.0).astype(BF16)
        o_intra = jnp.einsum("clm,cmv->clv", att, v, preferred_element_type=F32)
        pre.append((q_dec_all[:, :, ks], k_rem_all[:, :, ks], v, o_intra, decay_all[:, :, ks]))

    for s in range(nseg):
        @pl.when(is_sample)
        def _():
            for h in range(heads):
                state_ref[h] = s0_ref[s, h].T

        if s == 0:
            @pl.when(t == 0)
            def _():
                state_ref[...] = jnp.zeros_like(state_ref)

        for c in range(s * chunks_per_seg, (s + 1) * chunks_per_seg):
            for h in range(heads):
                q_dec, k_rem, v, o_intra, decay = pre[h]
                state = state_ref[h]
                o = o_intra[c] + lax.dot_general(q_dec[c], state.astype(BF16), NT_DIMS,
                                                 preferred_element_type=F32)
                kv = lax.dot_general(v[c], k_rem[c], TN_DIMS, preferred_element_type=F32)
                state_ref[h] = decay[c] * state + kv
                o_ref[c * cb:(c + 1) * cb, h * dv:(h + 1) * dv] = o

        @pl.when(is_sample)
        def _():
            for h in range(heads):
                souts_ref[s, h] = state_ref[h].T

        if s == nseg - 1:
            @pl.when(t == n_prompt_tiles - 1)
            def _():
                for h in range(heads):
                    soutp_ref[0, h] = state_ref[h].T

    g = g_ref[...]
    for h in range(heads):
        vs = slice(h * dv, (h + 1) * dv)
        r = lr_ref[:, vs]
        y_ref[:, vs] = (_rms(o_ref[:, vs], g) * (r * jax.nn.sigmoid(r))).astype(y_ref.dtype)


def gla(z, la, w_a2, b_a, g_gla, s0, layer, n_prompt_rows, seg_rows, col_q, col_k, col_v, col_r, tr=128):
    m = z.shape[0]
    _, nseq, heads, dk, dv = s0.shape
    kd, vd = heads * dk, heads * dv
    npt = n_prompt_rows // tr
    spt = tr // seg_rows
    assert n_prompt_rows % tr == 0 and tr % seg_rows == 0 and seg_rows % GLA_CHUNK == 0
    sample_tile = lambda t: (jnp.maximum(t - npt, 0), 0, 0, 0)
    wa2p = jnp.zeros((LA_PAD, kd), F32).at[:w_a2.shape[0]].set(w_a2)
    return pl.pallas_call(
        functools.partial(_gla_body, npt, seg_rows), grid=(m // tr,),
        in_specs=[pl.BlockSpec((tr, kd), lambda t: (t, col_q // kd)),
                  pl.BlockSpec((tr, kd), lambda t: (t, col_k // kd)),
                  pl.BlockSpec((tr, vd), lambda t: (t, col_v // vd)),
                  pl.BlockSpec((tr, vd), lambda t: (t, col_r // vd)),
                  pl.BlockSpec((tr, LA_PAD), lambda t: (t, 0)),
                  pl.BlockSpec((LA_PAD, kd), lambda t: (0, 0)),
                  pl.BlockSpec((1, kd), lambda t: (0, 0)),
                  pl.BlockSpec((1, dv), lambda t: (0, 0)),
                  pl.BlockSpec((None, spt, heads, dk, dv), lambda t: (layer,) + sample_tile(t))],
        out_specs=[pl.BlockSpec((tr, vd), lambda t: (t, 0)),
                   pl.BlockSpec((1, heads, dk, dv), lambda t: (0, 0, 0, 0)),
                   pl.BlockSpec((spt, heads, dk, dv), sample_tile)],
        out_shape=[jax.ShapeDtypeStruct((m, vd), BF16),
                   jax.ShapeDtypeStruct((1, heads, dk, dv), F32),
                   jax.ShapeDtypeStruct((nseq, heads, dk, dv), F32)],
        scratch_shapes=[pltpu.VMEM((heads, dv, dk), F32), pltpu.VMEM((tr, vd), F32)],
        compiler_params=_params(("arbitrary",), VMEM_LIMIT), name="gla",
    )(z, z, z, z, la, wa2p, b_a.reshape(1, -1), g_gla.reshape(1, -1), s0)


def _branch_body(n_prompt_tiles, ya_ref, ybp_ref, ybs_ref, yc_ref, wa_ref, wb_ref, wc_ref,
                 ga_ref, gb_ref, gc_ref, o_ref):
    yb = _pick_rows(pl.program_id(0), n_prompt_tiles, ybp_ref, ybs_ref)
    m = jax.nn.sigmoid(ga_ref[...]) * jnp.dot(ya_ref[...], wa_ref[...], preferred_element_type=F32)
    m = m + jax.nn.sigmoid(gb_ref[...]) * jnp.dot(yb, wb_ref[...], preferred_element_type=F32)
    m = m + jax.nn.sigmoid(gc_ref[...]) * jnp.dot(yc_ref[...], wc_ref[...], preferred_element_type=F32)
    o_ref[...] = m.astype(o_ref.dtype)


def branch_merge(ya, yb_p, yb_s, yc, w_branch, layer, z, gate_col, tm=1024, tn=512):
    m = ya.shape[0]
    ca, cb, cc = ya.shape[1], yb_p.shape[1], yc.shape[1]
    mp = yb_p.shape[0]
    d = w_branch.shape[2]
    assert ca == cb and cc % ca == 0 and (ca + cb) % cc == 0
    assert mp % tm == 0 and yb_s.shape[0] % tm == 0 and mp + yb_s.shape[0] == m
    gb = gate_col // tn
    nd = d // tn
    ybp_spec, ybs_spec = _split_row_specs(tm, cb, mp // tm, 0, lambda i, j: 0)
    return pl.pallas_call(
        functools.partial(_branch_body, mp // tm), grid=(m // tm, nd),
        in_specs=[pl.BlockSpec((tm, ca), lambda i, j: (i, 0)),
                  ybp_spec, ybs_spec,
                  pl.BlockSpec((tm, cc), lambda i, j: (i, 0)),
                  pl.BlockSpec((None, ca, tn), lambda i, j: (layer, 0, j)),
                  pl.BlockSpec((None, cb, tn), lambda i, j: (layer, 1, j)),
                  pl.BlockSpec((None, cc, tn), lambda i, j: (layer, (ca + cb) // cc, j)),
                  pl.BlockSpec((tm, tn), lambda i, j: (i, gb + j)),
                  pl.BlockSpec((tm, tn), lambda i, j: (i, gb + nd + j)),
                  pl.BlockSpec((tm, tn), lambda i, j: (i, gb + 2 * nd + j))],
        out_specs=pl.BlockSpec((tm, tn), lambda i, j: (i, j)),
        out_shape=jax.ShapeDtypeStruct((m, d), BF16),
        compiler_params=_params(("parallel", "parallel"), VMEM_LIMIT), name="branch_merge",
    )(ya, yb_p, yb_s, yc, w_branch, w_branch, w_branch, z, z, z)


def kernel(x_prompt, x_sample, cache_conv, cache_k, cache_v, state_gla, g_mix, w_in, conv_w, g_q, g_k,
           rel_bias, w_a2, b_a, g_gla, w_branch, w_out, g_ffn, w_gu, w_down):
    nb, seq, d = x_prompt.shape
    nd, ds, _ = x_sample.shape
    depth = g_mix.shape[0]
    assert nb == 1 and ds == CHUNK
    mp = nb * seq
    ms = nd * ds
    conv_dim = conv_w.shape[2]
    att_dim = ATT_HEADS * ATT_HEAD_DIM
    kdim = GLA_HEADS * GLA_DK
    vdim = GLA_HEADS * GLA_DV
    col_att = 3 * conv_dim
    col_lq = col_att + 3 * att_dim
    col_lk = col_lq + kdim
    col_lv = col_lk + kdim
    col_lr = col_lv + vdim
    col_la = col_lr + vdim
    col_gate = col_la
    keep = min(WINDOW, seq)

    xp = x_prompt.reshape(mp, d)
    xs = x_sample.reshape(ms, d)
    w_in_t = jnp.swapaxes(w_in, 1, 2)
    outs = {name: [] for name in ("conv_p", "k_p", "v_p", "gla_p", "conv_s", "k_s", "v_s", "gla_s")}
    for l in range(depth):
        h, la = rmsnorm_la(xp, xs, g_mix[l], w_in_t, l, col_la, GLA_RANK)
        z, w_branch_b, w_out_b = in_proj(h, w_in_t, l, col_la, GLA_RANK, w_branch, w_out)

        y_a, conv_tail = gated_conv(z, cache_conv[l], conv_w[l], mp)

        qn, kn, keys, vals = qk_norm(z, g_q[l], g_k[l], col_att // att_dim, ATT_HEADS)
        yb_p = prompt_attention(qn, keys, vals, rel_bias[l], mp)
        yb_s = sample_attention(qn, keys, vals, cache_k, cache_v, l, rel_bias[l], mp)

        y_c, s_prompt, s_sample = gla(z, la, w_a2[l], b_a[l], g_gla[l], state_gla, l, mp, ds,
                                      col_lq, col_lk, col_lv, col_lr)

        mrg = branch_merge(y_a, yb_p, yb_s, y_c, w_branch_b[None], 0, z, col_gate)
        xp, xs = matmul_residual(mrg, w_out_b[None], 0, xp, xs)

        hf = rmsnorm_bf16(xp, xs, g_ffn[l])
        hid, w_down_b = swiglu_up(hf, w_gu, l, w_down)
        xp, xs = matmul_residual(hid, w_down_b[None], 0, xp, xs, tm=512, tn=512)

        v_new = z[:, col_att + 2 * att_dim:col_att + 3 * att_dim]
        outs["conv_p"].append(conv_tail[mp // CHUNK - 1, 6:8][None])
        outs["conv_s"].append(conv_tail[mp // CHUNK:, 6:8])
        outs["k_p"].append(kn[mp - keep:mp].reshape(nb, keep, ATT_HEADS, ATT_HEAD_DIM))
        outs["v_p"].append(v_new[mp - keep:mp].reshape(nb, keep, ATT_HEADS, ATT_HEAD_DIM))
        outs["k_s"].append(kn[mp:].reshape(nd, ds, ATT_HEADS, ATT_HEAD_DIM))
        outs["v_s"].append(v_new[mp:].reshape(nd, ds, ATT_HEADS, ATT_HEAD_DIM))
        outs["gla_p"].append(s_prompt)
        outs["gla_s"].append(s_sample)

    st = {name: jnp.stack(v) for name, v in outs.items()}
    return (xp.reshape(nb, seq, d), xs.reshape(nd, ds, d),
            st["conv_p"], st["k_p"], st["v_p"], st["gla_p"],
            st["conv_s"], st["k_s"], st["v_s"], st["gla_s"])
```
